```python
import functools
import jax, jax.numpy as jnp
from jax import lax
import numpy as np

D_MODEL = 1024
BATCH = 2
SEQ = 8192
DEPTH = 4
DEC_BATCH = 128
DEC_SEQ = 1
PAST_LEN = 2048
PAGE_SIZE = 128

N_HEADS = 8
HEAD_DIM = 64
ATTN_WIDTH = N_HEADS * HEAD_DIM
Q_BLOCK = 128
SB_BIAS_INIT = -8.0
POOL_WINDOWS = (2, 4, 8, 16)
POOL_GROUPS = len(POOL_WINDOWS)
POOL_GROUP_DIM = 64
POOL_WIDTH = POOL_GROUPS * POOL_GROUP_DIM
POOL_PAD = max(POOL_WINDOWS) - 1
CONV_WIDTH = 256
CONV_SIZE = 31
CONV_PAD = CONV_SIZE - 1
N_BRANCHES = 3
D_FF = 2816
ALPHA = (2 * DEPTH) ** 0.25
BETA = (8 * DEPTH) ** -0.25
LN_EPS = 1e-5
Q_END = ATTN_WIDTH
K_END = 2 * ATTN_WIDTH
V_END = 3 * ATTN_WIDTH
U_END = V_END + POOL_WIDTH
C_END = U_END + 2 * CONV_WIDTH
N_IN = C_END + N_BRANCHES * D_MODEL

kernel_name = 'stickbreak_pool_conv_macaron_deepnorm_step'


def layer_norm(x, g, b):
    xf = x.astype(jnp.float32)
    mu = jnp.mean(xf, axis=-1, keepdims=True)
    var = jnp.mean(jnp.square(xf - mu), axis=-1, keepdims=True)
    return ((xf - mu) * lax.rsqrt(var + LN_EPS) * g + b).astype(x.dtype)


def swiglu(x, w_gu, w_down):
    a, b = jnp.split(x @ w_gu, 2, axis=-1)
    return (jax.nn.silu(a) * b) @ w_down


def stick_breaking(q, k, v, bias, q_pos, k_pos):
    z = jnp.einsum('bqhd,bkhd->bhqk', q.astype(jnp.float32), k.astype(jnp.float32)) * (HEAD_DIM ** -0.5)
    z = z + bias.astype(jnp.float32)[None, :, None, None]
    mask = k_pos[None, :] < q_pos[:, None]
    log_stay = jnp.where(mask, jax.nn.log_sigmoid(-z), 0.0)
    nxt = jnp.concatenate([log_stay[..., 1:], jnp.zeros_like(log_stay[..., :1])], axis=-1)
    suffix = lax.cumsum(nxt, axis=3, reverse=True)
    w = jnp.where(mask, jnp.exp(jax.nn.log_sigmoid(z) + suffix), 0.0)
    return jnp.einsum('bhqk,bkhd->bqhd', w, v.astype(jnp.float32)).astype(v.dtype)


def attend_prompt(q, k, v, bias):
    B, T, H, Dh = q.shape
    nb = T // Q_BLOCK
    qb = q.reshape(B, nb, Q_BLOCK, H, Dh).transpose(1, 0, 2, 3, 4)
    k_pos = jnp.arange(T)

    def one_block(args):
        q_blk, i = args
        q_pos = i * Q_BLOCK + jnp.arange(Q_BLOCK)
        return stick_breaking(q_blk, k, v, bias, q_pos, k_pos)

    out = lax.map(one_block, (qb, jnp.arange(nb)))
    return out.transpose(1, 0, 2, 3, 4).reshape(B, T, H, Dh)


def attend_with_past(q, k, v, bias, k_past, v_past):
    P = k_past.shape[1]
    T = q.shape[1]
    kk = jnp.concatenate([k_past, k], axis=1)
    vv = jnp.concatenate([v_past, v], axis=1)
    return stick_breaking(q, kk, vv, bias, P + jnp.arange(T), jnp.arange(P + T))


def multi_scale_pool(u, pool_prev, pos0, w_pool, pool_scale):
    B, T, _ = u.shape
    ext = jnp.concatenate([pool_prev, u], axis=1).astype(jnp.float32)
    cs = jnp.cumsum(jnp.pad(ext, ((0, 0), (1, 0), (0, 0))), axis=1)
    pos = pos0 + jnp.arange(T)
    cur = ext[:, POOL_PAD:]
    groups = []
    for gi, w in enumerate(POOL_WINDOWS):
        lo, hi = gi * POOL_GROUP_DIM, (gi + 1) * POOL_GROUP_DIM
        win_sum = cs[:, POOL_PAD + 1:, lo:hi] - cs[:, POOL_PAD + 1 - w:POOL_PAD + 1 - w + T, lo:hi]
        cnt = jnp.minimum(pos + 1, w).astype(jnp.float32)[None, :, None]
        groups.append(win_sum / cnt - cur[..., lo:hi])
    pooled = jnp.stack(groups, axis=2).astype(u.dtype)
    out = jnp.einsum('btgc,gcd->btgd', pooled, w_pool).reshape(B, T, D_MODEL)
    tail = jnp.concatenate([pool_prev, u], axis=1)[:, -POOL_PAD:]
    return out * pool_scale, tail


def conformer_conv(c, conv_prev, conv_w, conv_b, ln_g, ln_b, w_out):
    a, b = jnp.split(c, 2, axis=-1)
    glu = a * jax.nn.sigmoid(b)
    ext = jnp.concatenate([conv_prev, glu], axis=1)
    y = lax.conv_general_dilated(ext, conv_w, window_strides=(1,), padding='VALID',
                                 dimension_numbers=('NWC', 'WIO', 'NWC'),
                                 feature_group_count=CONV_WIDTH) + conv_b
    y = jax.nn.silu(layer_norm(y, ln_g, ln_b))
    return y @ w_out, ext[:, -CONV_PAD:]


def token_mix(x, p, pos0, pool_prev, conv_prev, attend):
    B, T, _ = x.shape
    h = x @ p['w_in']
    q = h[..., :Q_END].reshape(B, T, N_HEADS, HEAD_DIM)
    k = h[..., Q_END:K_END].reshape(B, T, N_HEADS, HEAD_DIM)
    v = h[..., K_END:V_END].reshape(B, T, N_HEADS, HEAD_DIM)
    u = h[..., V_END:U_END]
    c = h[..., U_END:C_END]
    gates = jax.nn.sigmoid(h[..., C_END:])
    g_a, g_p, g_c = jnp.split(gates, N_BRANCHES, axis=-1)
    br_a = attend(q, k, v, p['sb_bias']).reshape(B, T, ATTN_WIDTH) @ p['w_attn_out']
    br_p, pool_tail = multi_scale_pool(u, pool_prev, pos0, p['w_pool'], p['pool_scale'])
    br_c, conv_tail = conformer_conv(c, conv_prev, p['conv_w'], p['conv_b'],
                                     p['conv_ln_g'], p['conv_ln_b'], p['w_conv_out'])
    out = (g_a * br_a + g_p * br_p + g_c * br_c) @ p['w_o']
    return out, k, v, pool_tail, conv_tail


def layer(x, p, pos0, pool_prev, conv_prev, attend):
    x = layer_norm(ALPHA * x + 0.5 * swiglu(x, p['ffn1_gu'], p['ffn1_down']), p['ln1_g'], p['ln1_b'])
    m, k, v, pool_tail, conv_tail = token_mix(x, p, pos0, pool_prev, conv_prev, attend)
    x = layer_norm(ALPHA * x + m, p['ln2_g'], p['ln2_b'])
    x = layer_norm(ALPHA * x + 0.5 * swiglu(x, p['ffn2_gu'], p['ffn2_down']), p['ln3_g'], p['ln3_b'])
    return x, k, v, pool_tail, conv_tail


def setup_inputs(seed: int = 0) -> dict:
    key = jax.random.key(seed)
    ks = iter(jax.random.split(key, 40))
    f32 = jnp.float32
    n_pages = PAST_LEN // PAGE_SIZE
    n_used = DEC_BATCH * n_pages
    n_phys = n_used + n_used // 4

    def nrm(shape, scale):
        return scale * jax.random.normal(next(ks), shape, f32)

    def gain(shape):
        return 1.0 + nrm(shape, 0.02)

    x_prompt = nrm((BATCH, SEQ, D_MODEL), 1.0)
    x_sample = nrm((DEC_BATCH, DEC_SEQ, D_MODEL), 1.0)
    cache_k = nrm((DEPTH, n_phys, PAGE_SIZE, N_HEADS, HEAD_DIM), 1.0)
    cache_v = nrm((DEPTH, n_phys, PAGE_SIZE, N_HEADS, HEAD_DIM), 1.0)
    state_pool = nrm((DEPTH, DEC_BATCH, POOL_PAD, POOL_WIDTH), 1.0)
    state_conv = nrm((DEPTH, DEC_BATCH, CONV_PAD, CONV_WIDTH), 1.0)
    perm = jax.random.permutation(next(ks), n_phys)
    page_table = perm[:n_used].reshape(DEC_BATCH, n_pages).astype(jnp.int32)
    return {
        'x_prompt': x_prompt,
        'x_sample': x_sample,
        'cache_k': cache_k,
        'cache_v': cache_v,
        'state_pool': state_pool,
        'state_conv': state_conv,
        'page_table': page_table,
        'ln1_g': gain((DEPTH, D_MODEL)),
        'ln1_b': nrm((DEPTH, D_MODEL), 0.02),
        'ffn1_gu': nrm((DEPTH, D_MODEL, 2 * D_FF), D_MODEL ** -0.5),
        'ffn1_down': nrm((DEPTH, D_FF, D_MODEL), BETA * D_FF ** -0.5),
        'w_in': nrm((DEPTH, D_MODEL, N_IN), D_MODEL ** -0.5),
        'sb_bias': SB_BIAS_INIT + nrm((DEPTH, N_HEADS), 0.1),
        'w_attn_out': nrm((DEPTH, ATTN_WIDTH, D_MODEL), ATTN_WIDTH ** -0.5),
        'w_pool': nrm((DEPTH, POOL_GROUPS, POOL_GROUP_DIM, D_MODEL // POOL_GROUPS), POOL_GROUP_DIM ** -0.5),
        'pool_scale': gain((DEPTH, D_MODEL)),
        'conv_w': nrm((DEPTH, CONV_SIZE, 1, CONV_WIDTH), CONV_SIZE ** -0.5),
        'conv_b': nrm((DEPTH, CONV_WIDTH), 0.02),
        'conv_ln_g': gain((DEPTH, CONV_WIDTH)),
        'conv_ln_b': nrm((DEPTH, CONV_WIDTH), 0.02),
        'w_conv_out': nrm((DEPTH, CONV_WIDTH, D_MODEL), CONV_WIDTH ** -0.5),
        'w_o': nrm((DEPTH, D_MODEL, D_MODEL), BETA * D_MODEL ** -0.5),
        'ln2_g': gain((DEPTH, D_MODEL)),
        'ln2_b': nrm((DEPTH, D_MODEL), 0.02),
        'ffn2_gu': nrm((DEPTH, D_MODEL, 2 * D_FF), D_MODEL ** -0.5),
        'ffn2_down': nrm((DEPTH, D_FF, D_MODEL), BETA * D_FF ** -0.5),
        'ln3_g': gain((DEPTH, D_MODEL)),
        'ln3_b': nrm((DEPTH, D_MODEL), 0.02),
    }


def reference(x_prompt, x_sample, cache_k, cache_v, state_pool, state_conv, page_table,
              ln1_g, ln1_b, ffn1_gu, ffn1_down, w_in, sb_bias, w_attn_out, w_pool, pool_scale,
              conv_w, conv_b, conv_ln_g, conv_ln_b, w_conv_out, w_o, ln2_g, ln2_b,
              ffn2_gu, ffn2_down, ln3_g, ln3_b):
    n_seq, n_pages = page_table.shape
    page = cache_k.shape[2]
    past = n_pages * page
    B, T, _ = x_prompt.shape
    yp, ys = x_prompt, x_sample
    kp_l, vp_l, pp_l, cp_l = [], [], [], []
    ks_l, vs_l, ps_l, cs_l = [], [], [], []
    zero_pool = jnp.zeros((B, POOL_PAD, POOL_WIDTH), x_prompt.dtype)
    zero_conv = jnp.zeros((B, CONV_PAD, CONV_WIDTH), x_prompt.dtype)
    for l in range(DEPTH):
        p = dict(ln1_g=ln1_g[l], ln1_b=ln1_b[l], ffn1_gu=ffn1_gu[l], ffn1_down=ffn1_down[l],
                 w_in=w_in[l], sb_bias=sb_bias[l], w_attn_out=w_attn_out[l], w_pool=w_pool[l],
                 pool_scale=pool_scale[l], conv_w=conv_w[l], conv_b=conv_b[l],
                 conv_ln_g=conv_ln_g[l], conv_ln_b=conv_ln_b[l], w_conv_out=w_conv_out[l],
                 w_o=w_o[l], ln2_g=ln2_g[l], ln2_b=ln2_b[l],
                 ffn2_gu=ffn2_gu[l], ffn2_down=ffn2_down[l], ln3_g=ln3_g[l], ln3_b=ln3_b[l])
        yp, k_new, v_new, pool_tail, conv_tail = layer(yp, p, 0, zero_pool, zero_conv, attend_prompt)
        kp_l.append(k_new.reshape(B, T // PAGE_SIZE, PAGE_SIZE, N_HEADS, HEAD_DIM))
        vp_l.append(v_new.reshape(B, T // PAGE_SIZE, PAGE_SIZE, N_HEADS, HEAD_DIM))
        pp_l.append(pool_tail)
        cp_l.append(conv_tail)
        k_past = cache_k[l][page_table].reshape(n_seq, past, N_HEADS, HEAD_DIM)
        v_past = cache_v[l][page_table].reshape(n_seq, past, N_HEADS, HEAD_DIM)
        attend = functools.partial(attend_with_past, k_past=k_past, v_past=v_past)
        ys, k_new, v_new, pool_tail, conv_tail = layer(ys, p, past, state_pool[l], state_conv[l], attend)
        ks_l.append(k_new)
        vs_l.append(v_new)
        ps_l.append(pool_tail)
        cs_l.append(conv_tail)
    k_prompt = jnp.stack(kp_l)
    v_prompt = jnp.stack(vp_l)
    pool_prompt = jnp.stack(pp_l)
    conv_prompt = jnp.stack(cp_l)
    k_sample = jnp.stack(ks_l)
    v_sample = jnp.stack(vs_l)
    pool_sample = jnp.stack(ps_l)
    conv_sample = jnp.stack(cs_l)
    return (yp, ys, k_prompt, v_prompt, pool_prompt, conv_prompt, k_sample, v_sample, pool_sample, conv_sample)
```

```python
import functools

import jax
import jax.numpy as jnp
from jax import lax
from jax.experimental import pallas as pl
from jax.experimental.pallas import tpu as pltpu

F32 = jnp.float32
BF16 = jnp.bfloat16

LN_EPS = 1e-5
N_HEADS = 8
HEAD_DIM = 64
ATTN_WIDTH = N_HEADS * HEAD_DIM
POOL_WINDOWS = (2, 4, 8, 16)
POOL_GROUP_DIM = 64
POOL_WIDTH = len(POOL_WINDOWS) * POOL_GROUP_DIM
POOL_PAD = max(POOL_WINDOWS) - 1
CONV_WIDTH = 256
CONV_SIZE = 31
CONV_PAD = CONV_SIZE - 1
N_BRANCHES = 3
Q_END = ATTN_WIDTH
K_END = 2 * ATTN_WIDTH
V_END = 3 * ATTN_WIDTH
U_END = V_END + POOL_WIDTH
CA_END = U_END + CONV_WIDTH
C_END = U_END + 2 * CONV_WIDTH

LANES = 128
HEADS_PER_LANE_TILE = LANES // HEAD_DIM
HALO_ROWS = 32
ATTN_BLOCK = 256
SAMPLE_ROWS = 16
VMEM_LIMIT = 56 * 1024 * 1024


def _dot(a, b):
    return jnp.dot(a, b, preferred_element_type=F32)


def _dot_nt(a, b):
    return lax.dot_general(a, b, (((1,), (1,)), ((), ())), preferred_element_type=F32)


def _layer_norm(y, g, b):
    mu = jnp.mean(y, axis=-1, keepdims=True)
    d = y - mu
    var = jnp.mean(d * d, axis=-1, keepdims=True)
    return d * lax.rsqrt(var + LN_EPS) * g + b


def _softplus(z):
    return jnp.maximum(z, 0.0) + jnp.log1p(jnp.exp(-jnp.abs(z)))


def _suffix_matrix(n):
    r = lax.broadcasted_iota(jnp.int32, (n, n + LANES), 0)
    c = lax.broadcasted_iota(jnp.int32, (n, n + LANES), 1)
    return jnp.where((r >= c) | (c >= n), 1.0, 0.0).astype(BF16)


def _ff_chunks(d_ff):
    step = 1024
    return [(lo, min(lo + step, d_ff)) for lo in range(0, d_ff, step)]


def _ffn_ln(x, gu_ref, down_ref, g_ref, b_ref, alpha):
    d_ff = down_ref.shape[0]
    xb = x.astype(BF16)
    acc = None
    for lo, hi in _ff_chunks(d_ff):
        a = _dot(xb, gu_ref[:, lo:hi])
        b = _dot(xb, gu_ref[:, d_ff + lo:d_ff + hi])
        h = a * jax.nn.sigmoid(a) * b
        part = _dot(h.astype(BF16), down_ref[lo:hi, :])
        acc = part if acc is None else acc + part
    return _layer_norm(alpha * x + 0.5 * acc, g_ref[...], b_ref[...])


def _ffn_kernel(x_ref, gu_ref, down_ref, g_ref, b_ref, o_ref, *, alpha):
    o_ref[...] = _ffn_ln(x_ref[...], gu_ref, down_ref, g_ref, b_ref, alpha)


def _ffn_proj_kernel(x_ref, gu_ref, down_ref, g_ref, b_ref, wp_ref,
                     x1_ref, qs_ref, k_ref, kb_ref, v_ref, vb_ref, u_ref, glu_ref, *, alpha):
    x1 = _ffn_ln(x_ref[...], gu_ref, down_ref, g_ref, b_ref, alpha)
    x1_ref[...] = x1
    h = _dot(x1.astype(BF16), wp_ref[...])
    qs_ref[...] = (h[:, :Q_END] * (HEAD_DIM ** -0.5)).astype(BF16)
    k = h[:, Q_END:K_END]
    k_ref[...] = k
    kb_ref[...] = k.astype(BF16)
    v = h[:, K_END:V_END]
    v_ref[...] = v
    vb_ref[...] = v.astype(BF16)
    u_ref[...] = h[:, V_END:U_END]
    glu_ref[...] = h[:, U_END:CA_END] * jax.nn.sigmoid(h[:, CA_END:C_END])


def _resident(shape, layer):
    nd = len(shape)
    return pl.BlockSpec((None,) + tuple(shape[1:]), lambda *_: (layer,) + (0,) * (nd - 1),
                        pipeline_mode=pl.Buffered(1))


def _row_tile(n, want):
    t = min(n, want)
    assert n % t == 0 and (t % 8 == 0 or t == n)
    return t


def _ffn_call(x, w, layer, pre, alpha, with_proj):
    n, d = x.shape
    tm = _row_tile(n, 256)
    gu, down = w[pre + '_gu'], w[pre + '_down']
    ln = 'ln1' if pre == 'ffn1' else 'ln3'
    g, b = w[ln + '_g'], w[ln + '_b']
    row = lambda width: pl.BlockSpec((tm, width), lambda i: (i, 0))
    in_specs = [row(d), _resident(gu.shape, layer), _resident(down.shape, layer),
                _resident(g.shape, layer), _resident(b.shape, layer)]
    args = [x, gu, down, g, b]
    params = pltpu.CompilerParams(dimension_semantics=("parallel",), vmem_limit_bytes=VMEM_LIMIT)
    if not with_proj:
        return pl.pallas_call(
            functools.partial(_ffn_kernel, alpha=alpha),
            grid=(n // tm,), in_specs=in_specs, out_specs=row(d),
            out_shape=jax.ShapeDtypeStruct((n, d), F32), compiler_params=params,
            name=pre + '_ln')(*args)
    wp = w['w_proj']
    in_specs.append(_resident(wp.shape, layer))
    args.append(wp)
    widths = [(d, F32), (ATTN_WIDTH, BF16), (ATTN_WIDTH, F32), (ATTN_WIDTH, BF16),
              (ATTN_WIDTH, F32), (ATTN_WIDTH, BF16), (POOL_WIDTH, F32), (CONV_WIDTH, F32)]
    return pl.pallas_call(
        functools.partial(_ffn_proj_kernel, alpha=alpha),
        grid=(n // tm,), in_specs=in_specs,
        out_specs=[row(wd) for wd, _ in widths],
        out_shape=[jax.ShapeDtypeStruct((n, wd), dt) for wd, dt in widths],
        compiler_params=params, name=pre + '_ln_proj')(*args)


def _attn_prompt_kernel(bias_ref, q_ref, k_ref, v_ref, o_ref, carry_ref, acc_ref, *, blk):
    pair = pl.program_id(1)
    qi = pl.program_id(2)
    q = q_ref[...].astype(F32)
    lane = lax.broadcasted_iota(jnp.int32, (blk, LANES), 1)
    suffix = _suffix_matrix(blk)
    row = lax.broadcasted_iota(jnp.int32, (blk, blk), 0)
    col = lax.broadcasted_iota(jnp.int32, (blk, blk), 1)
    causal = col < row
    q_heads = [jnp.where((lane // HEAD_DIM) == hh, q, 0.0).astype(BF16)
               for hh in range(HEADS_PER_LANE_TILE)]
    biases = [bias_ref[pair * HEADS_PER_LANE_TILE + hh] for hh in range(HEADS_PER_LANE_TILE)]

    def block(start, masked, first):
        kj = k_ref[pl.ds(start, blk), :]
        vj = v_ref[pl.ds(start, blk), :]
        for hh in range(HEADS_PER_LANE_TILE):
            z = _dot_nt(q_heads[hh], kj) + biases[hh]
            sp = _softplus(z)
            if masked:
                sp = jnp.where(causal, sp, 0.0)
            su = _dot(sp.astype(BF16), suffix)
            within, total = su[:, :blk], su[:, blk:]
            if first:
                d = z - within
            else:
                c = carry_ref[hh]
                d = z - jnp.concatenate([c] * (blk // LANES), axis=1) - within
            wgt = jnp.exp(d)
            if masked:
                wgt = jnp.where(causal, wgt, 0.0)
            pv = _dot(wgt.astype(BF16), vj)
            if first:
                carry_ref[hh] = total
                acc_ref[hh] = pv
            else:
                carry_ref[hh] = c + total
                acc_ref[hh] = acc_ref[hh] + pv

    block(pl.multiple_of(qi * blk, blk), masked=True, first=True)

    def body(i, _):
        block(pl.multiple_of((qi - 1 - i) * blk, blk), masked=False, first=False)
        return 0

    lax.fori_loop(0, qi, body, 0)
    out = acc_ref[0]
    for hh in range(1, HEADS_PER_LANE_TILE):
        out = jnp.where((lane // HEAD_DIM) == hh, acc_ref[hh], out)
    o_ref[...] = out.astype(o_ref.dtype)


def _attn_prompt_call(qs, kb, vb, bias, batch, seq):
    blk = min(ATTN_BLOCK, seq)
    assert seq % blk == 0 and blk % LANES == 0
    nq = seq // blk
    n_tiles = ATTN_WIDTH // LANES
    q_spec = pl.BlockSpec((blk, LANES), lambda b, p, i: (b * nq + i, p))
    kv_spec = pl.BlockSpec((seq, LANES), lambda b, p, i: (b, p))
    return pl.pallas_call(
        functools.partial(_attn_prompt_kernel, blk=blk),
        grid=(batch, n_tiles, nq),
        in_specs=[pl.BlockSpec(memory_space=pltpu.SMEM), q_spec, kv_spec, kv_spec],
        out_specs=q_spec,
        out_shape=jax.ShapeDtypeStruct((batch * seq, ATTN_WIDTH), BF16),
        scratch_shapes=[pltpu.VMEM((HEADS_PER_LANE_TILE, blk, LANES), F32),
                        pltpu.VMEM((HEADS_PER_LANE_TILE, blk, LANES), F32)],
        compiler_params=pltpu.CompilerParams(
            dimension_semantics=("parallel", "parallel", "arbitrary"), vmem_limit_bytes=VMEM_LIMIT),
        name='attn_prompt')(bias, qs, kb, vb)


def _attn_sample_kernel(pt_ref, bias_ref, q_ref, *refs, n_pages):
    del pt_ref
    k_refs, v_refs, o_ref = refs[:n_pages], refs[n_pages:2 * n_pages], refs[2 * n_pages]
    page = k_refs[0].shape[0]
    row = lax.broadcasted_iota(jnp.int32, (SAMPLE_ROWS, ATTN_WIDTH), 0)
    lane = lax.broadcasted_iota(jnp.int32, (SAMPLE_ROWS, ATTN_WIDTH), 1)
    own = (lane // HEAD_DIM) == row
    q = jnp.broadcast_to(q_ref[...].astype(F32), (SAMPLE_ROWS, ATTN_WIDTH))
    q_rows = jnp.where(own, q, 0.0).astype(BF16)
    brow = lax.broadcasted_iota(jnp.int32, (SAMPLE_ROWS, page), 0)
    bias = jnp.zeros((SAMPLE_ROWS, page), F32)
    for h in range(N_HEADS):
        bias = jnp.where(brow == h, bias_ref[h], bias)
    suffix = _suffix_matrix(page)
    carry = None
    acc = None
    for p in range(n_pages):
        kp = k_refs[p][...].astype(BF16)
        vp = v_refs[p][...].astype(BF16)
        z = _dot_nt(q_rows, kp) + bias
        su = _dot(_softplus(z).astype(BF16), suffix)
        within, total = su[:, :page], su[:, page:]
        if carry is None:
            wgt = jnp.exp(z - within)
            carry = total
        else:
            wgt = jnp.exp(z - carry - within)
            carry = carry + total
        pv = _dot(wgt.astype(BF16), vp)
        acc = pv if acc is None else acc + pv
    out = jnp.sum(jnp.where(own, acc, 0.0), axis=0, keepdims=True)
    o_ref[...] = out.astype(o_ref.dtype)


def _attn_sample_call(qs, cache_k, cache_v, page_table, bias, layer):
    n_seq, n_pages = page_table.shape
    _, n_phys, page, _ = cache_k.shape
    assert page == LANES
    q3 = qs.reshape(n_seq, 1, ATTN_WIDTH)
    row_spec = pl.BlockSpec((None, 1, ATTN_WIDTH), lambda s, pt: (s, 0, 0))

    def page_spec(slot):
        return pl.BlockSpec((None, None, page, ATTN_WIDTH),
                            lambda s, pt: (layer, pt[s * n_pages + slot], 0, 0))

    slots = [n_pages - 1 - p for p in range(n_pages)]
    grid_spec = pltpu.PrefetchScalarGridSpec(
        num_scalar_prefetch=1, grid=(n_seq,),
        in_specs=[pl.BlockSpec(memory_space=pltpu.SMEM), row_spec]
        + [page_spec(s) for s in slots] * 2,
        out_specs=row_spec)
    out = pl.pallas_call(
        functools.partial(_attn_sample_kernel, n_pages=n_pages),
        grid_spec=grid_spec,
        out_shape=jax.ShapeDtypeStruct((n_seq, 1, ATTN_WIDTH), BF16),
        compiler_params=pltpu.CompilerParams(
            dimension_semantics=("parallel",), vmem_limit_bytes=VMEM_LIMIT),
        name='attn_sample')(page_table.reshape(-1), bias, q3,
                            *([cache_k] * n_pages), *([cache_v] * n_pages))
    return out.reshape(n_seq, ATTN_WIDTH)


def _pool_select(win2, win4, win8, win16, cur, count):
    group = lax.broadcasted_iota(jnp.int32, cur.shape, 1) // POOL_GROUP_DIM
    win = jnp.where(group == 0, win2, jnp.where(group == 1, win4, jnp.where(group == 2, win8, win16)))
    return win / count - cur


def _conv_post(y, cb_ref, cg_ref, cbeta_ref):
    y = _layer_norm(y + cb_ref[...], cg_ref[...], cbeta_ref[...])
    return y * jax.nn.sigmoid(y)


def _merge_out(x1, attn, pooled, conv_act, wao_ref, wpool_ref, ps_ref, wco_ref, wg_ref, wo_ref,
               g_ref, b_ref, alpha):
    d = x1.shape[1]
    x1b = x1.astype(BF16)
    group = lax.broadcasted_iota(jnp.int32, pooled.shape, 1) // POOL_GROUP_DIM
    wpool = wpool_ref[...]
    br_p = jnp.concatenate(
        [_dot(jnp.where(group == g, pooled, 0.0).astype(BF16), wpool)
         for g in range(len(POOL_WINDOWS))], axis=1) * ps_ref[...]
    merged = jax.nn.sigmoid(_dot(x1b, wg_ref[:, :d])) * _dot(attn, wao_ref[...])
    merged = merged + jax.nn.sigmoid(_dot(x1b, wg_ref[:, d:2 * d])) * br_p
    merged = merged + jax.nn.sigmoid(_dot(x1b, wg_ref[:, 2 * d:])) * _dot(conv_act.astype(BF16), wco_ref[...])
    m = _dot(merged.astype(BF16), wo_ref[...])
    return _layer_norm(alpha * x1 + m, g_ref[...], b_ref[...])


def _mix_prompt_kernel(x1_ref, attn_ref, u_ref, uh_ref, c_ref, ch_ref, cw_ref, cb_ref, cg_ref, cbeta_ref,
                       wao_ref, wpool_ref, ps_ref, wco_ref, wg_ref, wo_ref, g_ref, b_ref,
                       o_ref, uext_ref, cext_ref, *, alpha, tiles_per_seq):
    tm = x1_ref.shape[0]
    tile = pl.program_id(0) % tiles_per_seq
    keep = tile > 0
    uext_ref[0:HALO_ROWS, :] = jnp.where(keep, uh_ref[...], 0.0)
    uext_ref[HALO_ROWS:, :] = u_ref[...]
    cext_ref[0:HALO_ROWS, :] = jnp.where(keep, ch_ref[...], 0.0)
    cext_ref[HALO_ROWS:, :] = c_ref[...]

    def back(ref, k):
        return ref[HALO_ROWS - k:HALO_ROWS - k + tm, :]

    cur = u_ref[...]
    wins = {}
    win = cur
    for k in range(1, max(POOL_WINDOWS)):
        win = win + back(uext_ref, k)
        if k + 1 in POOL_WINDOWS:
            wins[k + 1] = win
    pos = tile * tm + lax.broadcasted_iota(jnp.int32, cur.shape, 0)
    width = jnp.left_shift(2, lax.broadcasted_iota(jnp.int32, cur.shape, 1) // POOL_GROUP_DIM)
    count = jnp.minimum(pos + 1, width).astype(F32)
    pooled = _pool_select(wins[2], wins[4], wins[8], wins[16], cur, count)

    y = None
    for j in range(CONV_SIZE):
        term = back(cext_ref, CONV_PAD - j) * cw_ref[j:j + 1, :]
        y = term if y is None else y + term
    conv_act = _conv_post(y, cb_ref, cg_ref, cbeta_ref)

    o_ref[...] = _merge_out(x1_ref[...], attn_ref[...], pooled, conv_act, wao_ref, wpool_ref, ps_ref,
                            wco_ref, wg_ref, wo_ref, g_ref, b_ref, alpha)


def _mix_sample_kernel(x1_ref, attn_ref, u_ref, sp_ref, c_ref, sc_ref, cw_ref, cb_ref, cg_ref, cbeta_ref,
                       wao_ref, wpool_ref, ps_ref, wco_ref, wg_ref, wo_ref, g_ref, b_ref,
                       o_ref, *, alpha, past):
    cur = u_ref[...]
    wins = {}
    win = cur
    for k in range(1, max(POOL_WINDOWS)):
        win = win + sp_ref[POOL_PAD - k]
        if k + 1 in POOL_WINDOWS:
            wins[k + 1] = win
    width = jnp.left_shift(2, lax.broadcasted_iota(jnp.int32, cur.shape, 1) // POOL_GROUP_DIM)
    count = jnp.minimum(past + 1, width).astype(F32)
    pooled = _pool_select(wins[2], wins[4], wins[8], wins[16], cur, count)

    y = c_ref[...] * cw_ref[CONV_PAD:CONV_SIZE, :]
    for j in range(CONV_PAD):
        y = y + sc_ref[j] * cw_ref[j:j + 1, :]
    conv_act = _conv_post(y, cb_ref, cg_ref, cbeta_ref)

    o_ref[...] = _merge_out(x1_ref[...], attn_ref[...], pooled, conv_act, wao_ref, wpool_ref, ps_ref,
                            wco_ref, wg_ref, wo_ref, g_ref, b_ref, alpha)


_MIX_WEIGHTS = ('conv_w', 'conv_b', 'conv_ln_g', 'conv_ln_b', 'w_attn_out', 'w_pool', 'pool_scale',
                'w_conv_out', 'w_gates', 'w_o', 'ln2_g', 'ln2_b')


def _mix_prompt_call(x1, attn, u, glu, w, layer, alpha, seq):
    n, d = x1.shape
    tm = _row_tile(seq, 256)
    assert tm % HALO_ROWS == 0 and HALO_ROWS >= CONV_PAD
    row = lambda width: pl.BlockSpec((tm, width), lambda i: (i, 0))
    halo = lambda width: pl.BlockSpec(
        (HALO_ROWS, width), lambda i: (jnp.maximum(i * (tm // HALO_ROWS) - 1, 0), 0))
    weights = [w[name] for name in _MIX_WEIGHTS]
    return pl.pallas_call(
        functools.partial(_mix_prompt_kernel, alpha=alpha, tiles_per_seq=seq // tm),
        grid=(n // tm,),
        in_specs=[row(d), row(ATTN_WIDTH), row(POOL_WIDTH), halo(POOL_WIDTH), row(CONV_WIDTH), halo(CONV_WIDTH)]
        + [_resident(x.shape, layer) for x in weights],
        out_specs=row(d),
        out_shape=jax.ShapeDtypeStruct((n, d), F32),
        scratch_shapes=[pltpu.VMEM((HALO_ROWS + tm, POOL_WIDTH), F32),
                        pltpu.VMEM((HALO_ROWS + tm, CONV_WIDTH), F32)],
        compiler_params=pltpu.CompilerParams(dimension_semantics=("parallel",), vmem_limit_bytes=VMEM_LIMIT),
        name='mix_prompt')(x1, attn, u, u, glu, glu, *weights)


def _mix_sample_call(x1, attn, u, state_pool_t, glu, state_conv_t, w, layer, alpha, past):
    n, d = x1.shape
    full = lambda width: pl.BlockSpec((n, width), lambda i: (0, 0))
    state = lambda x: pl.BlockSpec((None,) + x.shape[1:], lambda i: (layer, 0, 0, 0))
    weights = [w[name] for name in _MIX_WEIGHTS]
    return pl.pallas_call(
        functools.partial(_mix_sample_kernel, alpha=alpha, past=past),
        grid=(1,),
        in_specs=[full(d), full(ATTN_WIDTH), full(POOL_WIDTH), state(state_pool_t),
                  full(CONV_WIDTH), state(state_conv_t)]
        + [_resident(x.shape, layer) for x in weights],
        out_specs=full(d),
        out_shape=jax.ShapeDtypeStruct((n, d), F32),
        compiler_params=pltpu.CompilerParams(dimension_semantics=("arbitrary",), vmem_limit_bytes=VMEM_LIMIT),
        name='mix_sample')(x1, attn, u, state_pool_t, glu, state_conv_t, *weights)


def kernel(x_prompt, x_sample, cache_k, cache_v, state_pool, state_conv, page_table, ln1_g, ln1_b, ffn1_gu, ffn1_down, w_in, sb_bias, w_attn_out, w_pool, pool_scale, conv_w, conv_b, conv_ln_g, conv_ln_b, w_conv_out, w_o, ln2_g, ln2_b, ffn2_gu, ffn2_down, ln3_g, ln3_b):
    depth, d_model = ln1_g.shape
    batch, seq, _ = x_prompt.shape
    n_seq, dec_seq, _ = x_sample.shape
    assert dec_seq == 1
    n_pages = page_table.shape[1]
    n_phys, page = cache_k.shape[1], cache_k.shape[2]
    past = n_pages * page
    alpha = (2 * depth) ** 0.25
    vec = lambda a: a.reshape(depth, 1, -1)
    w = dict(
        ffn1_gu=ffn1_gu.astype(BF16), ffn1_down=ffn1_down.astype(BF16),
        ffn2_gu=ffn2_gu.astype(BF16), ffn2_down=ffn2_down.astype(BF16),
        w_proj=w_in[:, :, :C_END].astype(BF16), w_gates=w_in[:, :, C_END:].astype(BF16),
        w_attn_out=w_attn_out.astype(BF16), w_conv_out=w_conv_out.astype(BF16), w_o=w_o.astype(BF16),
        w_pool=w_pool.reshape(depth, POOL_WIDTH, -1).astype(BF16),
        conv_w=conv_w.reshape(depth, CONV_SIZE, CONV_WIDTH),
        ln1_g=vec(ln1_g), ln1_b=vec(ln1_b), ln2_g=vec(ln2_g), ln2_b=vec(ln2_b),
        ln3_g=vec(ln3_g), ln3_b=vec(ln3_b), pool_scale=vec(pool_scale),
        conv_b=vec(conv_b), conv_ln_g=vec(conv_ln_g), conv_ln_b=vec(conv_ln_b))
    cache_k4 = cache_k.reshape(depth, n_phys, page, ATTN_WIDTH)
    cache_v4 = cache_v.reshape(depth, n_phys, page, ATTN_WIDTH)
    state_pool_t = state_pool.transpose(0, 2, 1, 3)
    state_conv_t = state_conv.transpose(0, 2, 1, 3)

    yp = x_prompt.reshape(batch * seq, d_model)
    ys = x_sample.reshape(n_seq, d_model)
    outs = [[] for _ in range(8)]
    for l in range(depth):
        x1, qs, k, kb, v, vb, u, glu = _ffn_call(yp, w, l, 'ffn1', alpha, True)
        attn = _attn_prompt_call(qs, kb, vb, sb_bias[l], batch, seq)
        x2 = _mix_prompt_call(x1, attn, u, glu, w, l, alpha, seq)
        yp = _ffn_call(x2, w, l, 'ffn2', alpha, False)
        outs[0].append(k.reshape(batch, seq // page, page, N_HEADS, HEAD_DIM))
        outs[1].append(v.reshape(batch, seq // page, page, N_HEADS, HEAD_DIM))
        outs[2].append(u.reshape(batch, seq, POOL_WIDTH)[:, seq - POOL_PAD:])
        outs[3].append(glu.reshape(batch, seq, CONV_WIDTH)[:, seq - CONV_PAD:])

        x1, qs, k, _, v, _, u, glu = _ffn_call(ys, w, l, 'ffn1', alpha, True)
        attn = _attn_sample_call(qs, cache_k4, cache_v4, page_table, sb_bias[l], l)
        x2 = _mix_sample_call(x1, attn, u, state_pool_t, glu, state_conv_t, w, l, alpha, past)
        ys = _ffn_call(x2, w, l, 'ffn2', alpha, False)
        outs[4].append(k.reshape(n_seq, 1, N_HEADS, HEAD_DIM))
        outs[5].append(v.reshape(n_seq, 1, N_HEADS, HEAD_DIM))
        outs[6].append(jnp.concatenate([state_pool[l][:, 1:], u[:, None]], axis=1))
        outs[7].append(jnp.concatenate([state_conv[l][:, 1:], glu[:, None]], axis=1))
    stacked = [jnp.stack(o) for o in outs]
    return (yp.reshape(batch, seq, d_model), ys.reshape(n_seq, 1, d_model), *stacked)
```

```python
import functools

import jax
import jax.numpy as jnp
from jax import lax
from jax.experimental import pallas as pl
from jax.experimental.pallas import tpu as pltpu

F32 = jnp.float32
BF16 = jnp.bfloat16

LN_EPS = 1e-5
N_HEADS = 8
HEAD_DIM = 64
ATTN_WIDTH = N_HEADS * HEAD_DIM
POOL_WINDOWS = (2, 4, 8, 16)
POOL_GROUP_DIM = 64
POOL_WIDTH = len(POOL_WINDOWS) * POOL_GROUP_DIM
POOL_PAD = max(POOL_WINDOWS) - 1
CONV_WIDTH = 256
CONV_SIZE = 31
CONV_PAD = CONV_SIZE - 1
N_BRANCHES = 3
Q_END = ATTN_WIDTH
K_END = 2 * ATTN_WIDTH
V_END = 3 * ATTN_WIDTH
U_END = V_END + POOL_WIDTH
CA_END = U_END + CONV_WIDTH
C_END = U_END + 2 * CONV_WIDTH

LANES = 128
PAGE = 128
HEADS_PER_LANE_TILE = LANES // HEAD_DIM
HALO_ROWS = 32
ATTN_BLOCK = 256
ATTN_TILES_PER_STEP = 4
SAMPLE_ROWS = 16
VMEM_LIMIT = 56 * 1024 * 1024
LOG2E = 1.4426950408889634


def _dot(a, b):
    return jnp.dot(a, b, preferred_element_type=F32)


def _dot_nt(a, b):
    return lax.dot_general(a, b, (((1,), (1,)), ((), ())), preferred_element_type=F32)


def _layer_norm(y, g, b):
    mu = jnp.mean(y, axis=-1, keepdims=True)
    d = y - mu
    var = jnp.mean(d * d, axis=-1, keepdims=True)
    return d * lax.rsqrt(var + LN_EPS) * g + b


def _softplus2(z2):
    return jnp.where(z2 > 64.0, z2, jnp.log2(1.0 + jnp.exp2(z2)))


def _suffix_matrix(n):
    r = lax.broadcasted_iota(jnp.int32, (n, n + LANES), 0)
    c = lax.broadcasted_iota(jnp.int32, (n, n + LANES), 1)
    return jnp.where((r >= c) | (c >= n), 1.0, 0.0).astype(BF16)


def _ff_chunks(d_ff):
    step = 1024
    return [(lo, min(lo + step, d_ff)) for lo in range(0, d_ff, step)]


def _ffn_ln(x, gu_ref, down_ref, g_ref, b_ref, alpha):
    d_ff = down_ref.shape[0]
    xb = x.astype(BF16)
    acc = None
    for lo, hi in _ff_chunks(d_ff):
        a = _dot(xb, gu_ref[:, lo:hi])
        b = _dot(xb, gu_ref[:, d_ff + lo:d_ff + hi])
        h = a * jax.nn.sigmoid(a) * b
        part = _dot(h.astype(BF16), down_ref[lo:hi, :])
        acc = part if acc is None else acc + part
    return _layer_norm(alpha * x + 0.5 * acc, g_ref[...], b_ref[...])


def _ffn_kernel(x_ref, gu_ref, down_ref, g_ref, b_ref, o_ref, *, alpha):
    o_ref[...] = _ffn_ln(x_ref[...], gu_ref, down_ref, g_ref, b_ref, alpha)


def _ffn_proj_kernel(x_ref, gu_ref, down_ref, g_ref, b_ref, wp_ref,
                     x1_ref, qs_ref, kb_ref, vb_ref, kt_ref, vt_ref, u_ref, glu_ref, *, alpha):
    x1 = _ffn_ln(x_ref[...], gu_ref, down_ref, g_ref, b_ref, alpha)
    x1_ref[...] = x1
    h = _dot(x1.astype(BF16), wp_ref[...])
    qs_ref[...] = (h[:, :Q_END] * (LOG2E * HEAD_DIM ** -0.5)).astype(BF16)
    k = h[:, Q_END:K_END]
    kb_ref[...] = k.astype(BF16)
    k_t = k.T
    v_t = h[:, K_END:V_END].T
    vb_ref[...] = v_t.astype(BF16)
    for p in range(kt_ref.shape[0]):
        kt_ref[p] = k_t[:, p * PAGE:(p + 1) * PAGE]
        vt_ref[p] = v_t[:, p * PAGE:(p + 1) * PAGE]
    u_ref[...] = h[:, V_END:U_END]
    glu_ref[...] = h[:, U_END:CA_END] * jax.nn.sigmoid(h[:, CA_END:C_END])


def _resident(shape, layer):
    nd = len(shape)
    return pl.BlockSpec((None,) + tuple(shape[1:]), lambda *_: (layer,) + (0,) * (nd - 1),
                        pipeline_mode=pl.Buffered(1))


def _row_tile(n, want):
    t = min(n, want)
    assert n % t == 0 and (t % 8 == 0 or t == n)
    return t


def _ffn_call(x, w, layer, pre, alpha, with_proj):
    n, d = x.shape
    tm = _row_tile(n, 256)
    gu, down = w[pre + '_gu'], w[pre + '_down']
    ln = 'ln1' if pre == 'ffn1' else 'ln3'
    g, b = w[ln + '_g'], w[ln + '_b']
    row = lambda width: pl.BlockSpec((tm, width), lambda i: (i, 0))
    in_specs = [row(d), _resident(gu.shape, layer), _resident(down.shape, layer),
                _resident(g.shape, layer), _resident(b.shape, layer)]
    args = [x, gu, down, g, b]
    params = pltpu.CompilerParams(dimension_semantics=("parallel",), vmem_limit_bytes=VMEM_LIMIT)
    if not with_proj:
        return pl.pallas_call(
            functools.partial(_ffn_kernel, alpha=alpha),
            grid=(n // tm,), in_specs=in_specs, out_specs=row(d),
            out_shape=jax.ShapeDtypeStruct((n, d), F32), compiler_params=params,
            name=pre + '_ln')(*args)
    wp = w['w_proj']
    in_specs.append(_resident(wp.shape, layer))
    args.append(wp)
    assert tm % PAGE == 0
    pages = (pl.BlockSpec((tm // PAGE, ATTN_WIDTH, PAGE), lambda i: (i, 0, 0)),
             jax.ShapeDtypeStruct((n // PAGE, ATTN_WIDTH, PAGE), F32))
    rows = lambda width, dt: (row(width), jax.ShapeDtypeStruct((n, width), dt))
    outs = [rows(d, F32), rows(ATTN_WIDTH, BF16), rows(ATTN_WIDTH, BF16),
            (pl.BlockSpec((ATTN_WIDTH, tm), lambda i: (0, i)), jax.ShapeDtypeStruct((ATTN_WIDTH, n), BF16)),
            pages, pages, rows(POOL_WIDTH, F32), rows(CONV_WIDTH, F32)]
    return pl.pallas_call(
        functools.partial(_ffn_proj_kernel, alpha=alpha),
        grid=(n // tm,), in_specs=in_specs,
        out_specs=[o[0] for o in outs], out_shape=[o[1] for o in outs],
        compiler_params=params, name=pre + '_ln_proj')(*args)


def _attn_prompt_kernel(bias_ref, q_ref, k_ref, vt_ref, o_ref, acc_ref, *, blk, n_tiles):
    group = pl.program_id(1)
    qi = pl.program_id(2)
    hpt = HEADS_PER_LANE_TILE
    wide = hpt * blk
    lane = lax.broadcasted_iota(jnp.int32, (blk, LANES), 1)
    key = lax.broadcasted_iota(jnp.int32, (blk, blk), 0)
    qry = lax.broadcasted_iota(jnp.int32, (blk, blk), 1)
    causal = jnp.concatenate([key < qry] * hpt, axis=1)
    suffix_t = jnp.where(qry >= key, 1.0, 0.0).astype(BF16)
    col = lax.broadcasted_iota(jnp.int32, (1, wide), 1)
    tiles = range(n_tiles)
    q_cat, biases = [], []
    for t in tiles:
        q = q_ref[:, t * LANES:(t + 1) * LANES].astype(F32)
        q_cat.append(jnp.concatenate(
            [jnp.where((lane // HEAD_DIM) == hh, q, 0.0).astype(BF16) for hh in range(hpt)], axis=0))
        b = jnp.zeros((1, wide), F32)
        for hh in range(hpt):
            b = jnp.where((col >= hh * blk) & (col < (hh + 1) * blk),
                          bias_ref[(group * n_tiles + t) * hpt + hh] * LOG2E, b)
        biases.append(b)

    def block(start, carries):
        first = carries is None
        z2 = [_dot_nt(k_ref[pl.ds(start, blk), t * LANES:(t + 1) * LANES], q_cat[t]) + biases[t]
              for t in tiles]
        sp = [_softplus2(z) for z in z2]
        if first:
            sp = [jnp.where(causal, x, 0.0) for x in sp]
        base = [_dot(suffix_t, x.astype(BF16)) for x in sp]
        if not first:
            base = [x + c for x, c in zip(base, carries)]
        wgt = [jnp.exp2(z - x) for z, x in zip(z2, base)]
        if first:
            wgt = [jnp.where(causal, x, 0.0) for x in wgt]
        for t in tiles:
            pv = _dot(vt_ref[t * LANES:(t + 1) * LANES, pl.ds(start, blk)], wgt[t].astype(BF16))
            acc_ref[t] = pv if first else acc_ref[t] + pv
        return tuple(x[0:1, :] for x in base)

    carries = block(pl.multiple_of(qi * blk, blk), None)
    lax.fori_loop(0, qi, lambda i, c: block(pl.multiple_of((qi - 1 - i) * blk, blk), c), carries)
    for t in tiles:
        out_t = jnp.concatenate(
            [acc_ref[t, hh * HEAD_DIM:(hh + 1) * HEAD_DIM, hh * blk:(hh + 1) * blk] for hh in range(hpt)], axis=0)
        o_ref[:, t * LANES:(t + 1) * LANES] = out_t.T.astype(o_ref.dtype)


def _attn_prompt_call(qs, kb, vt, bias, batch, seq):
    blk = min(ATTN_BLOCK, seq)
    assert seq % blk == 0 and blk % LANES == 0
    nq = seq // blk
    n_tiles = ATTN_TILES_PER_STEP
    width = n_tiles * LANES
    q_spec = pl.BlockSpec((blk, width), lambda b, g, i: (b * nq + i, g))
    return pl.pallas_call(
        functools.partial(_attn_prompt_kernel, blk=blk, n_tiles=n_tiles),
        grid=(batch, ATTN_WIDTH // width, nq),
        in_specs=[pl.BlockSpec(memory_space=pltpu.SMEM), q_spec,
                  pl.BlockSpec((seq, width), lambda b, g, i: (b, g), pipeline_mode=pl.Buffered(1)),
                  pl.BlockSpec((width, seq), lambda b, g, i: (g, b), pipeline_mode=pl.Buffered(1))],
        out_specs=q_spec,
        out_shape=jax.ShapeDtypeStruct((batch * seq, ATTN_WIDTH), BF16),
        scratch_shapes=[pltpu.VMEM((n_tiles, LANES, HEADS_PER_LANE_TILE * blk), F32)],
        compiler_params=pltpu.CompilerParams(
            dimension_semantics=("parallel", "parallel", "arbitrary"), vmem_limit_bytes=VMEM_LIMIT),
        name='attn_prompt')(bias, qs, kb, vt)


def _attn_sample_kernel(pt_ref, bias_ref, q_ref, *refs, n_pages):
    del pt_ref
    k_refs, v_refs, o_ref = refs[:n_pages], refs[n_pages:2 * n_pages], refs[2 * n_pages]
    page = k_refs[0].shape[1]
    row = lax.broadcasted_iota(jnp.int32, (SAMPLE_ROWS, ATTN_WIDTH), 0)
    lane = lax.broadcasted_iota(jnp.int32, (SAMPLE_ROWS, ATTN_WIDTH), 1)
    own = (lane // HEAD_DIM) == row
    q = jnp.broadcast_to(q_ref[...].astype(F32), (SAMPLE_ROWS, ATTN_WIDTH))
    q_rows = jnp.where(own, q, 0.0).astype(BF16)
    brow = lax.broadcasted_iota(jnp.int32, (SAMPLE_ROWS, page), 0)
    bias = jnp.zeros((SAMPLE_ROWS, page), F32)
    for h in range(N_HEADS):
        bias = jnp.where(brow == h, bias_ref[h] * LOG2E, bias)
    z2 = jnp.concatenate([_dot(q_rows, k_refs[p][...].astype(BF16)) + bias
                          for p in range(n_pages)], axis=0)
    su = _dot(_softplus2(z2).astype(BF16), _suffix_matrix(page))
    within, total = su[:, :page], su[:, page:]
    carries = [jnp.zeros((SAMPLE_ROWS, page), F32)]
    for p in range(1, n_pages):
        carries.append(carries[-1] + total[(p - 1) * SAMPLE_ROWS:p * SAMPLE_ROWS])
    wgt = jnp.exp2(z2 - (within + jnp.concatenate(carries, axis=0))).astype(BF16)
    acc = None
    for p in range(n_pages):
        pv = _dot_nt(wgt[p * SAMPLE_ROWS:(p + 1) * SAMPLE_ROWS], v_refs[p][...].astype(BF16))
        acc = pv if acc is None else acc + pv
    out = jnp.sum(jnp.where(own, acc, 0.0), axis=0, keepdims=True)
    o_ref[...] = out.astype(o_ref.dtype)


def _attn_sample_call(qs, cache_k, cache_v, page_table, bias, layer):
    n_seq, n_pages = page_table.shape
    page = cache_k.shape[3]
    assert page == LANES
    q3 = qs.reshape(n_seq, 1, ATTN_WIDTH)
    row_spec = pl.BlockSpec((None, 1, ATTN_WIDTH), lambda s, pt: (s, 0, 0))

    def page_spec(slot):
        return pl.BlockSpec((None, None, ATTN_WIDTH, page),
                            lambda s, pt: (layer, pt[s * n_pages + slot], 0, 0))

    slots = [n_pages - 1 - p for p in range(n_pages)]
    grid_spec = pltpu.PrefetchScalarGridSpec(
        num_scalar_prefetch=1, grid=(n_seq,),
        in_specs=[pl.BlockSpec(memory_space=pltpu.SMEM), row_spec]
        + [page_spec(s) for s in slots] * 2,
        out_specs=row_spec)
    out = pl.pallas_call(
        functools.partial(_attn_sample_kernel, n_pages=n_pages),
        grid_spec=grid_spec,
        out_shape=jax.ShapeDtypeStruct((n_seq, 1, ATTN_WIDTH), BF16),
        compiler_params=pltpu.CompilerParams(
            dimension_semantics=("parallel",), vmem_limit_bytes=VMEM_LIMIT),
        name='attn_sample')(page_table.reshape(-1), bias, q3,
                            *([cache_k] * n_pages), *([cache_v] * n_pages))
    return out.reshape(n_seq, ATTN_WIDTH)


def _pool_select(win2, win4, win8, win16, cur, count):
    group = lax.broadcasted_iota(jnp.int32, cur.shape, 1) // POOL_GROUP_DIM
    win = jnp.where(group == 0, win2, jnp.where(group == 1, win4, jnp.where(group == 2, win8, win16)))
    return win / count - cur


def _conv_post(y, cb_ref, cg_ref, cbeta_ref):
    y = _layer_norm(y + cb_ref[...], cg_ref[...], cbeta_ref[...])
    return y * jax.nn.sigmoid(y)


def _merge_out(x1, attn, pooled, conv_act, wao_ref, wpool_ref, ps_ref, wco_ref, wg_ref, wo_ref,
               g_ref, b_ref, alpha):
    d = x1.shape[1]
    x1b = x1.astype(BF16)
    group = lax.broadcasted_iota(jnp.int32, pooled.shape, 1) // POOL_GROUP_DIM
    wpool = wpool_ref[...]
    br_p = jnp.concatenate(
        [_dot(jnp.where(group == g, pooled, 0.0).astype(BF16), wpool)
         for g in range(len(POOL_WINDOWS))], axis=1) * ps_ref[...]
    merged = jax.nn.sigmoid(_dot(x1b, wg_ref[:, :d])) * _dot(attn, wao_ref[...])
    merged = merged + jax.nn.sigmoid(_dot(x1b, wg_ref[:, d:2 * d])) * br_p
    merged = merged + jax.nn.sigmoid(_dot(x1b, wg_ref[:, 2 * d:])) * _dot(conv_act.astype(BF16), wco_ref[...])
    m = _dot(merged.astype(BF16), wo_ref[...])
    return _layer_norm(alpha * x1 + m, g_ref[...], b_ref[...])


def _mix_prompt_kernel(x1_ref, attn_ref, u_ref, uh_ref, c_ref, ch_ref, cw_ref, cb_ref, cg_ref, cbeta_ref,
                       wao_ref, wpool_ref, ps_ref, wco_ref, wg_ref, wo_ref, g_ref, b_ref,
                       o_ref, uext_ref, cext_ref, *, alpha, tiles_per_seq):
    tm = x1_ref.shape[0]
    tile = pl.program_id(0) % tiles_per_seq
    keep = tile > 0
    uext_ref[0:HALO_ROWS, :] = jnp.where(keep, uh_ref[...], 0.0)
    uext_ref[HALO_ROWS:, :] = u_ref[...]
    cext_ref[0:HALO_ROWS, :] = jnp.where(keep, ch_ref[...], 0.0)
    cext_ref[HALO_ROWS:, :] = c_ref[...]

    def back(ref, k):
        return ref[HALO_ROWS - k:HALO_ROWS - k + tm, :]

    cur = u_ref[...]
    wins = {}
    win = cur
    for k in range(1, max(POOL_WINDOWS)):
        win = win + back(uext_ref, k)
        if k + 1 in POOL_WINDOWS:
            wins[k + 1] = win
    pos = tile * tm + lax.broadcasted_iota(jnp.int32, cur.shape, 0)
    width = jnp.left_shift(2, lax.broadcasted_iota(jnp.int32, cur.shape, 1) // POOL_GROUP_DIM)
    count = jnp.minimum(pos + 1, width).astype(F32)
    pooled = _pool_select(wins[2], wins[4], wins[8], wins[16], cur, count)

    y = None
    for j in range(CONV_SIZE):
        term = back(cext_ref, CONV_PAD - j) * cw_ref[j:j + 1, :]
        y = term if y is None else y + term
    conv_act = _conv_post(y, cb_ref, cg_ref, cbeta_ref)

    o_ref[...] = _merge_out(x1_ref[...], attn_ref[...], pooled, conv_act, wao_ref, wpool_ref, ps_ref,
                            wco_ref, wg_ref, wo_ref, g_ref, b_ref, alpha)


def _mix_sample_kernel(x1_ref, attn_ref, u_ref, sp_ref, c_ref, sc_ref, cw_ref, cb_ref, cg_ref, cbeta_ref,
                       wao_ref, wpool_ref, ps_ref, wco_ref, wg_ref, wo_ref, g_ref, b_ref,
                       o_ref, *, alpha, past):
    cur = u_ref[...]
    wins = {}
    win = cur
    for k in range(1, max(POOL_WINDOWS)):
        win = win + sp_ref[POOL_PAD - k]
        if k + 1 in POOL_WINDOWS:
            wins[k + 1] = win
    width = jnp.left_shift(2, lax.broadcasted_iota(jnp.int32, cur.shape, 1) // POOL_GROUP_DIM)
    count = jnp.minimum(past + 1, width).astype(F32)
    pooled = _pool_select(wins[2], wins[4], wins[8], wins[16], cur, count)

    y = c_ref[...] * cw_ref[CONV_PAD:CONV_SIZE, :]
    for j in range(CONV_PAD):
        y = y + sc_ref[j] * cw_ref[j:j + 1, :]
    conv_act = _conv_post(y, cb_ref, cg_ref, cbeta_ref)

    o_ref[...] = _merge_out(x1_ref[...], attn_ref[...], pooled, conv_act, wao_ref, wpool_ref, ps_ref,
                            wco_ref, wg_ref, wo_ref, g_ref, b_ref, alpha)


_MIX_WEIGHTS = ('conv_w', 'conv_b', 'conv_ln_g', 'conv_ln_b', 'w_attn_out', 'w_pool', 'pool_scale',
                'w_conv_out', 'w_gates', 'w_o', 'ln2_g', 'ln2_b')


def _mix_prompt_call(x1, attn, u, glu, w, layer, alpha, seq):
    n, d = x1.shape
    tm = _row_tile(seq, 256)
    assert tm % HALO_ROWS == 0 and HALO_ROWS >= CONV_PAD
    row = lambda width: pl.BlockSpec((tm, width), lambda i: (i, 0))
    halo = lambda width: pl.BlockSpec(
        (HALO_ROWS, width), lambda i: (jnp.maximum(i * (tm // HALO_ROWS) - 1, 0), 0))
    weights = [w[name] for name in _MIX_WEIGHTS]
    return pl.pallas_call(
        functools.partial(_mix_prompt_kernel, alpha=alpha, tiles_per_seq=seq // tm),
        grid=(n // tm,),
        in_specs=[row(d), row(ATTN_WIDTH), row(POOL_WIDTH), halo(POOL_WIDTH), row(CONV_WIDTH), halo(CONV_WIDTH)]
        + [_resident(x.shape, layer) for x in weights],
        out_specs=row(d),
        out_shape=jax.ShapeDtypeStruct((n, d), F32),
        scratch_shapes=[pltpu.VMEM((HALO_ROWS + tm, POOL_WIDTH), F32),
                        pltpu.VMEM((HALO_ROWS + tm, CONV_WIDTH), F32)],
        compiler_params=pltpu.CompilerParams(dimension_semantics=("parallel",), vmem_limit_bytes=VMEM_LIMIT),
        name='mix_prompt')(x1, attn, u, u, glu, glu, *weights)


def _mix_sample_call(x1, attn, u, state_pool_t, glu, state_conv_t, w, layer, alpha, past):
    n, d = x1.shape
    full = lambda width: pl.BlockSpec((n, width), lambda i: (0, 0))
    state = lambda x: pl.BlockSpec((None,) + x.shape[1:], lambda i: (layer, 0, 0, 0))
    weights = [w[name] for name in _MIX_WEIGHTS]
    return pl.pallas_call(
        functools.partial(_mix_sample_kernel, alpha=alpha, past=past),
        grid=(1,),
        in_specs=[full(d), full(ATTN_WIDTH), full(POOL_WIDTH), state(state_pool_t),
                  full(CONV_WIDTH), state(state_conv_t)]
        + [_resident(x.shape, layer) for x in weights],
        out_specs=full(d),
        out_shape=jax.ShapeDtypeStruct((n, d), F32),
        compiler_params=pltpu.CompilerParams(dimension_semantics=("arbitrary",), vmem_limit_bytes=VMEM_LIMIT),
        name='mix_sample')(x1, attn, u, state_pool_t, glu, state_conv_t, *weights)


def kernel(x_prompt, x_sample, cache_k, cache_v, state_pool, state_conv, page_table, ln1_g, ln1_b, ffn1_gu, ffn1_down, w_in, sb_bias, w_attn_out, w_pool, pool_scale, conv_w, conv_b, conv_ln_g, conv_ln_b, w_conv_out, w_o, ln2_g, ln2_b, ffn2_gu, ffn2_down, ln3_g, ln3_b):
    depth, d_model = ln1_g.shape
    batch, seq, _ = x_prompt.shape
    n_seq, dec_seq, _ = x_sample.shape
    assert dec_seq == 1 and n_seq == cache_k.shape[2] == PAGE
    n_pages = page_table.shape[1]
    n_phys, page = cache_k.shape[1], cache_k.shape[2]
    past = n_pages * page
    alpha = (2 * depth) ** 0.25
    vec = lambda a: a.reshape(depth, 1, -1)
    w = dict(
        ffn1_gu=ffn1_gu.astype(BF16), ffn1_down=ffn1_down.astype(BF16),
        ffn2_gu=ffn2_gu.astype(BF16), ffn2_down=ffn2_down.astype(BF16),
        w_proj=w_in[:, :, :C_END].astype(BF16), w_gates=w_in[:, :, C_END:].astype(BF16),
        w_attn_out=w_attn_out.astype(BF16), w_conv_out=w_conv_out.astype(BF16), w_o=w_o.astype(BF16),
        w_pool=w_pool.reshape(depth, POOL_WIDTH, -1).astype(BF16),
        conv_w=conv_w.reshape(depth, CONV_SIZE, CONV_WIDTH),
        ln1_g=vec(ln1_g), ln1_b=vec(ln1_b), ln2_g=vec(ln2_g), ln2_b=vec(ln2_b),
        ln3_g=vec(ln3_g), ln3_b=vec(ln3_b), pool_scale=vec(pool_scale),
        conv_b=vec(conv_b), conv_ln_g=vec(conv_ln_g), conv_ln_b=vec(conv_ln_b))
    cache_kt = cache_k.transpose(0, 1, 3, 4, 2).reshape(depth, n_phys, ATTN_WIDTH, page)
    cache_vt = cache_v.transpose(0, 1, 3, 4, 2).reshape(depth, n_phys, ATTN_WIDTH, page)
    state_pool_t = state_pool.transpose(0, 2, 1, 3)
    state_conv_t = state_conv.transpose(0, 2, 1, 3)

    def pages_out(x_t, lead):
        x = x_t.reshape(*lead, N_HEADS, HEAD_DIM, page)
        return jnp.moveaxis(x, -1, -3)

    yp = x_prompt.reshape(batch * seq, d_model)
    ys = x_sample.reshape(n_seq, d_model)
    outs = [[] for _ in range(8)]
    for l in range(depth):
        x1, qs, kb, vb, kt, vt, u, glu = _ffn_call(yp, w, l, 'ffn1', alpha, True)
        attn = _attn_prompt_call(qs, kb, vb, sb_bias[l], batch, seq)
        x2 = _mix_prompt_call(x1, attn, u, glu, w, l, alpha, seq)
        yp = _ffn_call(x2, w, l, 'ffn2', alpha, False)
        outs[0].append(pages_out(kt, (batch, seq // page)))
        outs[1].append(pages_out(vt, (batch, seq // page)))
        outs[2].append(u.reshape(batch, seq, POOL_WIDTH)[:, seq - POOL_PAD:])
        outs[3].append(glu.reshape(batch, seq, CONV_WIDTH)[:, seq - CONV_PAD:])

        x1, qs, _, _, kt, vt, u, glu = _ffn_call(ys, w, l, 'ffn1', alpha, True)
        attn = _attn_sample_call(qs, cache_kt, cache_vt, page_table, sb_bias[l], l)
        x2 = _mix_sample_call(x1, attn, u, state_pool_t, glu, state_conv_t, w, l, alpha, past)
        ys = _ffn_call(x2, w, l, 'ffn2', alpha, False)
        outs[4].append(pages_out(kt, (1,)).reshape(n_seq, 1, N_HEADS, HEAD_DIM))
        outs[5].append(pages_out(vt, (1,)).reshape(n_seq, 1, N_HEADS, HEAD_DIM))
        outs[6].append(jnp.concatenate([state_pool[l][:, 1:], u[:, None]], axis=1))
        outs[7].append(jnp.concatenate([state_conv[l][:, 1:], glu[:, None]], axis=1))
    stacked = [jnp.stack(o) for o in outs]
    return (yp.reshape(batch, seq, d_model), ys.reshape(n_seq, 1, d_model), *stacked)
```

```python
import functools

import jax
import jax.numpy as jnp
from jax import lax
from jax.experimental import pallas as pl
from jax.experimental.pallas import tpu as pltpu

F32 = jnp.float32
BF16 = jnp.bfloat16

LN_EPS = 1e-5
N_HEADS = 8
HEAD_DIM = 64
ATTN_WIDTH = N_HEADS * HEAD_DIM
POOL_WINDOWS = (2, 4, 8, 16)
POOL_GROUP_DIM = 64
POOL_WIDTH = len(POOL_WINDOWS) * POOL_GROUP_DIM
POOL_PAD = max(POOL_WINDOWS) - 1
CONV_WIDTH = 256
CONV_SIZE = 31
CONV_PAD = CONV_SIZE - 1
N_BRANCHES = 3
Q_END = ATTN_WIDTH
K_END = 2 * ATTN_WIDTH
V_END = 3 * ATTN_WIDTH
U_END = V_END + POOL_WIDTH
CA_END = U_END + CONV_WIDTH
C_END = U_END + 2 * CONV_WIDTH

LANES = 128
PAGE = 128
HEADS_PER_LANE_TILE = LANES // HEAD_DIM
HALO_ROWS = 32
ATTN_BLOCK = 256
ATTN_TILES_PER_STEP = 4
ATTN_BLOCKS_PER_ITER = 2
ATTN_HELD_CHAINS = 2
SAMPLE_ROWS = 16
VMEM_LIMIT = 56 * 1024 * 1024
LOG2E = 1.4426950408889634


def _dot(a, b):
    return jnp.dot(a, b, preferred_element_type=F32)


def _dot_nt(a, b):
    return lax.dot_general(a, b, (((1,), (1,)), ((), ())), preferred_element_type=F32)


def _layer_norm(y, g, b):
    mu = jnp.mean(y, axis=-1, keepdims=True)
    d = y - mu
    var = jnp.mean(d * d, axis=-1, keepdims=True)
    return d * lax.rsqrt(var + LN_EPS) * g + b


def _softplus2(z2):
    return jnp.where(z2 > 64.0, z2, jnp.log2(1.0 + jnp.exp2(z2)))


def _suffix_matrix(n):
    r = lax.broadcasted_iota(jnp.int32, (n, n + LANES), 0)
    c = lax.broadcasted_iota(jnp.int32, (n, n + LANES), 1)
    return jnp.where((r >= c) | (c >= n), 1.0, 0.0).astype(BF16)


def _ff_chunks(d_ff):
    step = 1024
    return [(lo, min(lo + step, d_ff)) for lo in range(0, d_ff, step)]


def _ffn_ln(x, gu_ref, down_ref, g_ref, b_ref, alpha):
    d_ff = down_ref.shape[0]
    xb = x.astype(BF16)
    acc = None
    for lo, hi in _ff_chunks(d_ff):
        a = _dot(xb, gu_ref[:, lo:hi])
        b = _dot(xb, gu_ref[:, d_ff + lo:d_ff + hi])
        h = a * jax.nn.sigmoid(a) * b
        part = _dot(h.astype(BF16), down_ref[lo:hi, :])
        acc = part if acc is None else acc + part
    return _layer_norm(alpha * x + 0.5 * acc, g_ref[...], b_ref[...])


def _ffn_kernel(x_ref, gu_ref, down_ref, g_ref, b_ref, o_ref, *, alpha):
    o_ref[...] = _ffn_ln(x_ref[...], gu_ref, down_ref, g_ref, b_ref, alpha)


def _ffn_proj_kernel(x_ref, gu_ref, down_ref, g_ref, b_ref, wp_ref,
                     x1_ref, qs_ref, kb_ref, vb_ref, kt_ref, vt_ref, u_ref, glu_ref, *, alpha):
    x1 = _ffn_ln(x_ref[...], gu_ref, down_ref, g_ref, b_ref, alpha)
    x1_ref[...] = x1
    h = _dot(x1.astype(BF16), wp_ref[...])
    qs_ref[...] = (h[:, :Q_END] * (LOG2E * HEAD_DIM ** -0.5)).astype(BF16)
    k = h[:, Q_END:K_END]
    kb_ref[...] = k.astype(BF16)
    k_t = k.T
    v_t = h[:, K_END:V_END].T
    vb_ref[...] = v_t.astype(BF16)
    for p in range(kt_ref.shape[0]):
        kt_ref[p] = k_t[:, p * PAGE:(p + 1) * PAGE]
        vt_ref[p] = v_t[:, p * PAGE:(p + 1) * PAGE]
    u_ref[...] = h[:, V_END:U_END]
    glu_ref[...] = h[:, U_END:CA_END] * jax.nn.sigmoid(h[:, CA_END:C_END])


def _resident(shape, layer):
    nd = len(shape)
    return pl.BlockSpec((None,) + tuple(shape[1:]), lambda *_: (layer,) + (0,) * (nd - 1),
                        pipeline_mode=pl.Buffered(1))


def _row_tile(n, want):
    t = min(n, want)
    assert n % t == 0 and (t % 8 == 0 or t == n)
    return t


def _ffn_call(x, w, layer, pre, alpha, with_proj):
    n, d = x.shape
    tm = _row_tile(n, 256)
    gu, down = w[pre + '_gu'], w[pre + '_down']
    ln = 'ln1' if pre == 'ffn1' else 'ln3'
    g, b = w[ln + '_g'], w[ln + '_b']
    row = lambda width: pl.BlockSpec((tm, width), lambda i: (i, 0))
    in_specs = [row(d), _resident(gu.shape, layer), _resident(down.shape, layer),
                _resident(g.shape, layer), _resident(b.shape, layer)]
    args = [x, gu, down, g, b]
    params = pltpu.CompilerParams(dimension_semantics=("parallel",), vmem_limit_bytes=VMEM_LIMIT)
    if not with_proj:
        return pl.pallas_call(
            functools.partial(_ffn_kernel, alpha=alpha),
            grid=(n // tm,), in_specs=in_specs, out_specs=row(d),
            out_shape=jax.ShapeDtypeStruct((n, d), F32), compiler_params=params,
            name=pre + '_ln')(*args)
    wp = w['w_proj']
    in_specs.append(_resident(wp.shape, layer))
    args.append(wp)
    assert tm % PAGE == 0
    pages = (pl.BlockSpec((tm // PAGE, ATTN_WIDTH, PAGE), lambda i: (i, 0, 0)),
             jax.ShapeDtypeStruct((n // PAGE, ATTN_WIDTH, PAGE), F32))
    rows = lambda width, dt: (row(width), jax.ShapeDtypeStruct((n, width), dt))
    outs = [rows(d, F32), rows(ATTN_WIDTH, BF16), rows(ATTN_WIDTH, BF16),
            (pl.BlockSpec((ATTN_WIDTH, tm), lambda i: (0, i)), jax.ShapeDtypeStruct((ATTN_WIDTH, n), BF16)),
            pages, pages, rows(POOL_WIDTH, F32), rows(CONV_WIDTH, F32)]
    return pl.pallas_call(
        functools.partial(_ffn_proj_kernel, alpha=alpha),
        grid=(n // tm,), in_specs=in_specs,
        out_specs=[o[0] for o in outs], out_shape=[o[1] for o in outs],
        compiler_params=params, name=pre + '_ln_proj')(*args)


def _attn_prompt_kernel(bias_ref, q_ref, k_ref, vt_ref, o_ref, acc_ref, zheld_ref, bheld_ref, *, blk, n_tiles):
    group = pl.program_id(1)
    qi = pl.program_id(2)
    hpt = HEADS_PER_LANE_TILE
    wide = hpt * blk
    lane = lax.broadcasted_iota(jnp.int32, (blk, LANES), 1)
    key = lax.broadcasted_iota(jnp.int32, (blk, blk), 0)
    qry = lax.broadcasted_iota(jnp.int32, (blk, blk), 1)
    causal = jnp.concatenate([key < qry] * hpt, axis=1)
    suffix_t = jnp.where(qry >= key, 1.0, 0.0).astype(BF16)
    ones_cols = jnp.where(lane < 3, 1.0, 0.0).astype(BF16)
    q_ext = []
    for t in range(n_tiles):
        q = q_ref[:, t * LANES:(t + 1) * LANES].astype(F32)
        rows = []
        for hh in range(hpt):
            b = bias_ref[(group * n_tiles + t) * hpt + hh] * LOG2E
            b0 = b.astype(BF16).astype(F32)
            b1 = (b - b0).astype(BF16).astype(F32)
            b2 = (b - b0 - b1).astype(BF16).astype(F32)
            offs = jnp.where(lane == 0, b0, jnp.where(lane == 1, b1, jnp.where(lane == 2, b2, 0.0)))
            rows.append(jnp.concatenate([jnp.where((lane // HEAD_DIM) == hh, q, 0.0), offs], axis=1))
        q_ext.append(jnp.concatenate(rows, axis=0).astype(BF16))

    def scores(chain):
        start, t = chain
        kj = k_ref[pl.ds(start, blk), t * LANES:(t + 1) * LANES]
        return _dot_nt(jnp.concatenate([kj, ones_cols], axis=1), q_ext[t])

    def softplus(z2, masked):
        sp = _softplus2(z2)
        if masked:
            sp = jnp.where(causal, sp, 0.0)
        return sp.astype(BF16)

    def values(chain, z2, base, masked):
        start, t = chain
        wgt = jnp.exp2(z2 - base)
        if masked:
            wgt = jnp.where(causal, wgt, 0.0)
        pv = _dot(vt_ref[t * LANES:(t + 1) * LANES, pl.ds(start, blk)], wgt.astype(BF16))
        acc_ref[t] = pv if masked else acc_ref[t] + pv

    held = list(range(n_tiles - ATTN_HELD_CHAINS, n_tiles))

    def finish_held(start):
        for n, t in enumerate(held):
            values((start, t), zheld_ref[n], bheld_ref[n], False)

    def run(starts, carries):
        first = carries is None
        chains = [(start, t) for start in starts for t in range(n_tiles)]
        carries = [None] * n_tiles if first else list(carries)
        n_now = len(chains) if first else len(chains) - len(held)
        z2, base = [scores(chains[0]), scores(chains[1])], []
        if not first:
            finish_held(starts[0] + blk)
        for c, (start, t) in enumerate(chains):
            if 0 < c and c + 1 < len(chains):
                z2.append(scores(chains[c + 1]))
            within = _dot(suffix_t, softplus(z2[c], first))
            base.append(within if carries[t] is None else within + carries[t])
            carries[t] = base[c][0:1, :]
            if 0 < c <= n_now:
                values(chains[c - 1], z2[c - 1], base[c - 1], first)
        if first:
            values(chains[-1], z2[-1], base[-1], first)
        for n in range(len(chains) - n_now):
            zheld_ref[n] = z2[n_now + n]
            bheld_ref[n] = base[n_now + n]
        return tuple(carries)

    def start_of(j):
        return pl.multiple_of(j * blk, blk)

    carries = run([start_of(qi)], None)
    for n in range(len(held)):
        zheld_ref[n] = jnp.full((blk, wide), -1e30, F32)
        bheld_ref[n] = jnp.zeros((blk, wide), F32)
    per = ATTN_BLOCKS_PER_ITER
    carries = lax.fori_loop(
        0, qi // per,
        lambda i, c: run([start_of(qi - 1 - i * per - r) for r in range(per)], c), carries)
    lax.fori_loop(0, qi % per, lambda i, c: run([start_of(qi % per - 1 - i)], c), carries)
    finish_held(0)
    for t in range(n_tiles):
        out_t = jnp.concatenate(
            [acc_ref[t, hh * HEAD_DIM:(hh + 1) * HEAD_DIM, hh * blk:(hh + 1) * blk] for hh in range(hpt)], axis=0)
        o_ref[:, t * LANES:(t + 1) * LANES] = out_t.T.astype(o_ref.dtype)


def _attn_prompt_call(qs, kb, vt, bias, batch, seq):
    blk = min(ATTN_BLOCK, seq)
    assert seq % blk == 0 and blk % LANES == 0
    nq = seq // blk
    n_tiles = ATTN_TILES_PER_STEP
    width = n_tiles * LANES
    q_spec = pl.BlockSpec((blk, width), lambda b, g, i: (b * nq + i, g))
    return pl.pallas_call(
        functools.partial(_attn_prompt_kernel, blk=blk, n_tiles=n_tiles),
        grid=(batch, ATTN_WIDTH // width, nq),
        in_specs=[pl.BlockSpec(memory_space=pltpu.SMEM), q_spec,
                  pl.BlockSpec((seq, width), lambda b, g, i: (b, g), pipeline_mode=pl.Buffered(1)),
                  pl.BlockSpec((width, seq), lambda b, g, i: (g, b), pipeline_mode=pl.Buffered(1))],
        out_specs=q_spec,
        out_shape=jax.ShapeDtypeStruct((batch * seq, ATTN_WIDTH), BF16),
        scratch_shapes=[pltpu.VMEM((n_tiles, LANES, HEADS_PER_LANE_TILE * blk), F32),
                        pltpu.VMEM((ATTN_HELD_CHAINS, blk, HEADS_PER_LANE_TILE * blk), F32),
                        pltpu.VMEM((ATTN_HELD_CHAINS, blk, HEADS_PER_LANE_TILE * blk), F32)],
        compiler_params=pltpu.CompilerParams(
            dimension_semantics=("parallel", "parallel", "arbitrary"), vmem_limit_bytes=VMEM_LIMIT),
        name='attn_prompt')(bias, qs, kb, vt)


def _attn_sample_kernel(pt_ref, bias_ref, q_ref, *refs, n_pages):
    del pt_ref
    k_refs, v_refs, o_ref = refs[:n_pages], refs[n_pages:2 * n_pages], refs[2 * n_pages]
    page = k_refs[0].shape[1]
    row = lax.broadcasted_iota(jnp.int32, (SAMPLE_ROWS, ATTN_WIDTH), 0)
    lane = lax.broadcasted_iota(jnp.int32, (SAMPLE_ROWS, ATTN_WIDTH), 1)
    own = (lane // HEAD_DIM) == row
    q = jnp.broadcast_to(q_ref[...].astype(F32), (SAMPLE_ROWS, ATTN_WIDTH))
    q_rows = jnp.where(own, q, 0.0).astype(BF16)
    brow = lax.broadcasted_iota(jnp.int32, (SAMPLE_ROWS, page), 0)
    bias = jnp.zeros((SAMPLE_ROWS, page), F32)
    for h in range(N_HEADS):
        bias = jnp.where(brow == h, bias_ref[h] * LOG2E, bias)
    z2 = jnp.concatenate([_dot(q_rows, k_refs[p][...].astype(BF16)) + bias
                          for p in range(n_pages)], axis=0)
    su = _dot(_softplus2(z2).astype(BF16), _suffix_matrix(page))
    within, total = su[:, :page], su[:, page:]
    carries = [jnp.zeros((SAMPLE_ROWS, page), F32)]
    for p in range(1, n_pages):
        carries.append(carries[-1] + total[(p - 1) * SAMPLE_ROWS:p * SAMPLE_ROWS])
    wgt = jnp.exp2(z2 - (within + jnp.concatenate(carries, axis=0))).astype(BF16)
    acc = None
    for p in range(n_pages):
        pv = _dot_nt(wgt[p * SAMPLE_ROWS:(p + 1) * SAMPLE_ROWS], v_refs[p][...].astype(BF16))
        acc = pv if acc is None else acc + pv
    out = jnp.sum(jnp.where(own, acc, 0.0), axis=0, keepdims=True)
    o_ref[...] = out.astype(o_ref.dtype)


def _attn_sample_call(qs, cache_k, cache_v, page_table, bias, layer):
    n_seq, n_pages = page_table.shape
    page = cache_k.shape[3]
    assert page == LANES
    q3 = qs.reshape(n_seq, 1, ATTN_WIDTH)
    row_spec = pl.BlockSpec((None, 1, ATTN_WIDTH), lambda s, pt: (s, 0, 0))

    def page_spec(slot):
        return pl.BlockSpec((None, None, ATTN_WIDTH, page),
                            lambda s, pt: (layer, pt[s * n_pages + slot], 0, 0))

    slots = [n_pages - 1 - p for p in range(n_pages)]
    grid_spec = pltpu.PrefetchScalarGridSpec(
        num_scalar_prefetch=1, grid=(n_seq,),
        in_specs=[pl.BlockSpec(memory_space=pltpu.SMEM), row_spec]
        + [page_spec(s) for s in slots] * 2,
        out_specs=row_spec)
    out = pl.pallas_call(
        functools.partial(_attn_sample_kernel, n_pages=n_pages),
        grid_spec=grid_spec,
        out_shape=jax.ShapeDtypeStruct((n_seq, 1, ATTN_WIDTH), BF16),
        compiler_params=pltpu.CompilerParams(
            dimension_semantics=("parallel",), vmem_limit_bytes=VMEM_LIMIT),
        name='attn_sample')(page_table.reshape(-1), bias, q3,
                            *([cache_k] * n_pages), *([cache_v] * n_pages))
    return out.reshape(n_seq, ATTN_WIDTH)


def _pool_select(win2, win4, win8, win16, cur, count):
    group = lax.broadcasted_iota(jnp.int32, cur.shape, 1) // POOL_GROUP_DIM
    win = jnp.where(group == 0, win2, jnp.where(group == 1, win4, jnp.where(group == 2, win8, win16)))
    return win / count - cur


def _conv_post(y, cb_ref, cg_ref, cbeta_ref):
    y = _layer_norm(y + cb_ref[...], cg_ref[...], cbeta_ref[...])
    return y * jax.nn.sigmoid(y)


def _merge_out(x1, attn, pooled, conv_act, wao_ref, wpool_ref, ps_ref, wco_ref, wg_ref, wo_ref,
               g_ref, b_ref, alpha):
    d = x1.shape[1]
    x1b = x1.astype(BF16)
    group = lax.broadcasted_iota(jnp.int32, pooled.shape, 1) // POOL_GROUP_DIM
    wpool = wpool_ref[...]
    br_p = jnp.concatenate(
        [_dot(jnp.where(group == g, pooled, 0.0).astype(BF16), wpool)
         for g in range(len(POOL_WINDOWS))], axis=1) * ps_ref[...]
    merged = jax.nn.sigmoid(_dot(x1b, wg_ref[:, :d])) * _dot(attn, wao_ref[...])
    merged = merged + jax.nn.sigmoid(_dot(x1b, wg_ref[:, d:2 * d])) * br_p
    merged = merged + jax.nn.sigmoid(_dot(x1b, wg_ref[:, 2 * d:])) * _dot(conv_act.astype(BF16), wco_ref[...])
    m = _dot(merged.astype(BF16), wo_ref[...])
    return _layer_norm(alpha * x1 + m, g_ref[...], b_ref[...])


def _mix_prompt_kernel(x1_ref, attn_ref, u_ref, uh_ref, c_ref, ch_ref, cw_ref, cb_ref, cg_ref, cbeta_ref,
                       wao_ref, wpool_ref, ps_ref, wco_ref, wg_ref, wo_ref, g_ref, b_ref,
                       o_ref, uext_ref, cext_ref, ushift_ref, cshift_ref, *, alpha, tiles_per_seq):
    tm = x1_ref.shape[0]
    tile = pl.program_id(0) % tiles_per_seq
    keep = tile > 0
    uext_ref[0:HALO_ROWS, :] = jnp.where(keep, uh_ref[...], 0.0)
    uext_ref[HALO_ROWS:, :] = u_ref[...]
    cext_ref[0:HALO_ROWS, :] = jnp.where(keep, ch_ref[...], 0.0)
    cext_ref[HALO_ROWS:, :] = c_ref[...]

    def shifted(ext_ref, copies_ref):
        for b in range(8):
            copies_ref[b] = ext_ref[8 - b:HALO_ROWS + tm - b, :]
        return copies_ref

    def back(copies_ref, k):
        a, b = divmod(k, 8)
        off = HALO_ROWS - 8 - 8 * a
        return copies_ref[b, off:off + tm, :]

    cur = u_ref[...]
    u_back = shifted(uext_ref, ushift_ref)
    wins = {}
    win = cur
    for k in range(1, max(POOL_WINDOWS)):
        win = win + back(u_back, k)
        if k + 1 in POOL_WINDOWS:
            wins[k + 1] = win
    pos = tile * tm + lax.broadcasted_iota(jnp.int32, cur.shape, 0)
    width = jnp.left_shift(2, lax.broadcasted_iota(jnp.int32, cur.shape, 1) // POOL_GROUP_DIM)
    count = jnp.minimum(pos + 1, width).astype(F32)
    pooled = _pool_select(wins[2], wins[4], wins[8], wins[16], cur, count)

    c_back = shifted(cext_ref, cshift_ref)
    y = None
    for j in range(CONV_SIZE):
        term = back(c_back, CONV_PAD - j) * cw_ref[j:j + 1, :]
        y = term if y is None else y + term
    conv_act = _conv_post(y, cb_ref, cg_ref, cbeta_ref)

    o_ref[...] = _merge_out(x1_ref[...], attn_ref[...], pooled, conv_act, wao_ref, wpool_ref, ps_ref,
                            wco_ref, wg_ref, wo_ref, g_ref, b_ref, alpha)


def _mix_sample_kernel(x1_ref, attn_ref, u_ref, sp_ref, c_ref, sc_ref, cw_ref, cb_ref, cg_ref, cbeta_ref,
                       wao_ref, wpool_ref, ps_ref, wco_ref, wg_ref, wo_ref, g_ref, b_ref,
                       o_ref, *, alpha, past):
    cur = u_ref[...]
    wins = {}
    win = cur
    for k in range(1, max(POOL_WINDOWS)):
        win = win + sp_ref[POOL_PAD - k]
        if k + 1 in POOL_WINDOWS:
            wins[k + 1] = win
    width = jnp.left_shift(2, lax.broadcasted_iota(jnp.int32, cur.shape, 1) // POOL_GROUP_DIM)
    count = jnp.minimum(past + 1, width).astype(F32)
    pooled = _pool_select(wins[2], wins[4], wins[8], wins[16], cur, count)

    y = c_ref[...] * cw_ref[CONV_PAD:CONV_SIZE, :]
    for j in range(CONV_PAD):
        y = y + sc_ref[j] * cw_ref[j:j + 1, :]
    conv_act = _conv_post(y, cb_ref, cg_ref, cbeta_ref)

    o_ref[...] = _merge_out(x1_ref[...], attn_ref[...], pooled, conv_act, wao_ref, wpool_ref, ps_ref,
                            wco_ref, wg_ref, wo_ref, g_ref, b_ref, alpha)


_MIX_WEIGHTS = ('conv_w', 'conv_b', 'conv_ln_g', 'conv_ln_b', 'w_attn_out', 'w_pool', 'pool_scale',
                'w_conv_out', 'w_gates', 'w_o', 'ln2_g', 'ln2_b')


def _mix_prompt_call(x1, attn, u, glu, w, layer, alpha, seq):
    n, d = x1.shape
    tm = _row_tile(seq, 256)
    assert tm % HALO_ROWS == 0 and HALO_ROWS >= CONV_PAD
    row = lambda width: pl.BlockSpec((tm, width), lambda i: (i, 0))
    halo = lambda width: pl.BlockSpec(
        (HALO_ROWS, width), lambda i: (jnp.maximum(i * (tm // HALO_ROWS) - 1, 0), 0))
    weights = [w[name] for name in _MIX_WEIGHTS]
    return pl.pallas_call(
        functools.partial(_mix_prompt_kernel, alpha=alpha, tiles_per_seq=seq // tm),
        grid=(n // tm,),
        in_specs=[row(d), row(ATTN_WIDTH), row(POOL_WIDTH), halo(POOL_WIDTH), row(CONV_WIDTH), halo(CONV_WIDTH)]
        + [_resident(x.shape, layer) for x in weights],
        out_specs=row(d),
        out_shape=jax.ShapeDtypeStruct((n, d), F32),
        scratch_shapes=[pltpu.VMEM((HALO_ROWS + tm, POOL_WIDTH), F32),
                        pltpu.VMEM((HALO_ROWS + tm, CONV_WIDTH), F32),
                        pltpu.VMEM((8, HALO_ROWS + tm - 8, POOL_WIDTH), F32),
                        pltpu.VMEM((8, HALO_ROWS + tm - 8, CONV_WIDTH), F32)],
        compiler_params=pltpu.CompilerParams(dimension_semantics=("parallel",), vmem_limit_bytes=VMEM_LIMIT),
        name='mix_prompt')(x1, attn, u, u, glu, glu, *weights)


def _mix_sample_call(x1, attn, u, state_pool_t, glu, state_conv_t, w, layer, alpha, past):
    n, d = x1.shape
    full = lambda width: pl.BlockSpec((n, width), lambda i: (0, 0))
    state = lambda x: pl.BlockSpec((None,) + x.shape[1:], lambda i: (layer, 0, 0, 0))
    weights = [w[name] for name in _MIX_WEIGHTS]
    return pl.pallas_call(
        functools.partial(_mix_sample_kernel, alpha=alpha, past=past),
        grid=(1,),
        in_specs=[full(d), full(ATTN_WIDTH), full(POOL_WIDTH), state(state_pool_t),
                  full(CONV_WIDTH), state(state_conv_t)]
        + [_resident(x.shape, layer) for x in weights],
        out_specs=full(d),
        out_shape=jax.ShapeDtypeStruct((n, d), F32),
        compiler_params=pltpu.CompilerParams(dimension_semantics=("arbitrary",), vmem_limit_bytes=VMEM_LIMIT),
        name='mix_sample')(x1, attn, u, state_pool_t, glu, state_conv_t, *weights)


def kernel(x_prompt, x_sample, cache_k, cache_v, state_pool, state_conv, page_table, ln1_g, ln1_b, ffn1_gu, ffn1_down, w_in, sb_bias, w_attn_out, w_pool, pool_scale, conv_w, conv_b, conv_ln_g, conv_ln_b, w_conv_out, w_o, ln2_g, ln2_b, ffn2_gu, ffn2_down, ln3_g, ln3_b):
    depth, d_model = ln1_g.shape
    batch, seq, _ = x_prompt.shape
    n_seq, dec_seq, _ = x_sample.shape
    assert dec_seq == 1 and n_seq == cache_k.shape[2] == PAGE
    n_pages = page_table.shape[1]
    n_phys, page = cache_k.shape[1], cache_k.shape[2]
    past = n_pages * page
    alpha = (2 * depth) ** 0.25
    vec = lambda a: a.reshape(depth, 1, -1)
    w = dict(
        ffn1_gu=ffn1_gu.astype(BF16), ffn1_down=ffn1_down.astype(BF16),
        ffn2_gu=ffn2_gu.astype(BF16), ffn2_down=ffn2_down.astype(BF16),
        w_proj=w_in[:, :, :C_END].astype(BF16), w_gates=w_in[:, :, C_END:].astype(BF16),
        w_attn_out=w_attn_out.astype(BF16), w_conv_out=w_conv_out.astype(BF16), w_o=w_o.astype(BF16),
        w_pool=w_pool.reshape(depth, POOL_WIDTH, -1).astype(BF16),
        conv_w=conv_w.reshape(depth, CONV_SIZE, CONV_WIDTH),
        ln1_g=vec(ln1_g), ln1_b=vec(ln1_b), ln2_g=vec(ln2_g), ln2_b=vec(ln2_b),
        ln3_g=vec(ln3_g), ln3_b=vec(ln3_b), pool_scale=vec(pool_scale),
        conv_b=vec(conv_b), conv_ln_g=vec(conv_ln_g), conv_ln_b=vec(conv_ln_b))
    cache_kt = cache_k.transpose(0, 1, 3, 4, 2).reshape(depth, n_phys, ATTN_WIDTH, page)
    cache_vt = cache_v.transpose(0, 1, 3, 4, 2).reshape(depth, n_phys, ATTN_WIDTH, page)
    state_pool_t = state_pool.transpose(0, 2, 1, 3)
    state_conv_t = state_conv.transpose(0, 2, 1, 3)

    def pages_out(x_t, lead):
        x = x_t.reshape(*lead, N_HEADS, HEAD_DIM, page)
        return jnp.moveaxis(x, -1, -3)

    yp = x_prompt.reshape(batch * seq, d_model)
    ys = x_sample.reshape(n_seq, d_model)
    outs = [[] for _ in range(8)]
    for l in range(depth):
        x1, qs, kb, vb, kt, vt, u, glu = _ffn_call(yp, w, l, 'ffn1', alpha, True)
        attn = _attn_prompt_call(qs, kb, vb, sb_bias[l], batch, seq)
        x2 = _mix_prompt_call(x1, attn, u, glu, w, l, alpha, seq)
        yp = _ffn_call(x2, w, l, 'ffn2', alpha, False)
        outs[0].append(pages_out(kt, (batch, seq // page)))
        outs[1].append(pages_out(vt, (batch, seq // page)))
        outs[2].append(u.reshape(batch, seq, POOL_WIDTH)[:, seq - POOL_PAD:])
        outs[3].append(glu.reshape(batch, seq, CONV_WIDTH)[:, seq - CONV_PAD:])

        x1, qs, _, _, kt, vt, u, glu = _ffn_call(ys, w, l, 'ffn1', alpha, True)
        attn = _attn_sample_call(qs, cache_kt, cache_vt, page_table, sb_bias[l], l)
        x2 = _mix_sample_call(x1, attn, u, state_pool_t, glu, state_conv_t, w, l, alpha, past)
        ys = _ffn_call(x2, w, l, 'ffn2', alpha, False)
        outs[4].append(pages_out(kt, (1,)).reshape(n_seq, 1, N_HEADS, HEAD_DIM))
        outs[5].append(pages_out(vt, (1,)).reshape(n_seq, 1, N_HEADS, HEAD_DIM))
        outs[6].append(jnp.concatenate([state_pool[l][:, 1:], u[:, None]], axis=1))
        outs[7].append(jnp.concatenate([state_conv[l][:, 1:], glu[:, None]], axis=1))
    stacked = [jnp.stack(o) for o in outs]
    return (yp.reshape(batch, seq, d_model), ys.reshape(n_seq, 1, d_model), *stacked)
```

```python
import functools

import jax
import jax.numpy as jnp
from jax import lax
from jax.experimental import pallas as pl
from jax.experimental.pallas import tpu as pltpu

F32 = jnp.float32
BF16 = jnp.bfloat16

LN_EPS = 1e-5
N_HEADS = 8
HEAD_DIM = 64
ATTN_WIDTH = N_HEADS * HEAD_DIM
POOL_WINDOWS = (2, 4, 8, 16)
POOL_GROUP_DIM = 64
POOL_WIDTH = len(POOL_WINDOWS) * POOL_GROUP_DIM
POOL_PAD = max(POOL_WINDOWS) - 1
CONV_WIDTH = 256
CONV_SIZE = 31
CONV_PAD = CONV_SIZE - 1
N_BRANCHES = 3
Q_END = ATTN_WIDTH
K_END = 2 * ATTN_WIDTH
V_END = 3 * ATTN_WIDTH
U_END = V_END + POOL_WIDTH
CA_END = U_END + CONV_WIDTH
C_END = U_END + 2 * CONV_WIDTH

LANES = 128
PAGE = 128
HEADS_PER_LANE_TILE = LANES // HEAD_DIM
HALO_ROWS = 32
ATTN_BLOCK = 256
ATTN_TILES_PER_STEP = 4
ATTN_BLOCKS_PER_ITER = 4
SAMPLE_ROWS = 16
VMEM_LIMIT = 56 * 1024 * 1024
LOG2E = 1.4426950408889634


def _dot(a, b):
    return jnp.dot(a, b, preferred_element_type=F32)


def _dot_nt(a, b):
    return lax.dot_general(a, b, (((1,), (1,)), ((), ())), preferred_element_type=F32)


def _layer_norm(y, g, b):
    mu = jnp.mean(y, axis=-1, keepdims=True)
    d = y - mu
    var = jnp.mean(d * d, axis=-1, keepdims=True)
    return d * lax.rsqrt(var + LN_EPS) * g + b


def _softplus2(z2):
    return jnp.where(z2 > 64.0, z2, jnp.log2(1.0 + jnp.exp2(z2)))


def _suffix_matrix(n):
    r = lax.broadcasted_iota(jnp.int32, (n, n + LANES), 0)
    c = lax.broadcasted_iota(jnp.int32, (n, n + LANES), 1)
    return jnp.where((r >= c) | (c >= n), 1.0, 0.0).astype(BF16)


def _ff_chunks(d_ff):
    step = 1024
    return [(lo, min(lo + step, d_ff)) for lo in range(0, d_ff, step)]


def _ffn_ln(x, gu_ref, down_ref, g_ref, b_ref, alpha):
    d_ff = down_ref.shape[0]
    xb = x.astype(BF16)
    acc = None
    for lo, hi in _ff_chunks(d_ff):
        a = _dot(xb, gu_ref[:, lo:hi])
        b = _dot(xb, gu_ref[:, d_ff + lo:d_ff + hi])
        h = a * jax.nn.sigmoid(a) * b
        part = _dot(h.astype(BF16), down_ref[lo:hi, :])
        acc = part if acc is None else acc + part
    return _layer_norm(alpha * x + 0.5 * acc, g_ref[...], b_ref[...])


def _ffn_kernel(x_ref, gu_ref, down_ref, g_ref, b_ref, o_ref, *, alpha):
    o_ref[...] = _ffn_ln(x_ref[...], gu_ref, down_ref, g_ref, b_ref, alpha)


def _ffn_proj_kernel(x_ref, gu_ref, down_ref, g_ref, b_ref, wp_ref,
                     x1_ref, qs_ref, qn_ref, kb_ref, vb_ref, kt_ref, vt_ref, u_ref, glu_ref, *, alpha):
    x1 = _ffn_ln(x_ref[...], gu_ref, down_ref, g_ref, b_ref, alpha)
    x1_ref[...] = x1
    h = _dot(x1.astype(BF16), wp_ref[...])
    qs_ref[...] = (h[:, :Q_END] * (LOG2E * HEAD_DIM ** -0.5)).astype(BF16)
    qn_ref[...] = (h[:, :Q_END] * (-0.5 * HEAD_DIM ** -0.5)).astype(BF16)
    k = h[:, Q_END:K_END]
    k_t = k.T
    v_t = h[:, K_END:V_END].T
    tm = k.shape[0]
    r = lax.broadcasted_iota(jnp.int32, (tm, tm), 0)
    c = lax.broadcasted_iota(jnp.int32, (tm, tm), 1)
    perm = jnp.where(c == (r & 7) * (tm // 8) + (r >> 3), 1.0, 0.0).astype(BF16)
    perm_t = jnp.where(r == (c & 7) * (tm // 8) + (c >> 3), 1.0, 0.0).astype(BF16)
    kb_ref[...] = _dot(perm, k.astype(BF16)).astype(BF16)
    vb_ref[...] = _dot(v_t.astype(BF16), perm_t).astype(BF16)
    for p in range(kt_ref.shape[0]):
        kt_ref[p] = k_t[:, p * PAGE:(p + 1) * PAGE]
        vt_ref[p] = v_t[:, p * PAGE:(p + 1) * PAGE]
    u_ref[...] = h[:, V_END:U_END]
    glu_ref[...] = h[:, U_END:CA_END] * jax.nn.sigmoid(h[:, CA_END:C_END])


def _resident(shape, layer):
    nd = len(shape)
    return pl.BlockSpec((None,) + tuple(shape[1:]), lambda *_: (layer,) + (0,) * (nd - 1),
                        pipeline_mode=pl.Buffered(1))


def _row_tile(n, want):
    t = min(n, want)
    assert n % t == 0 and (t % 8 == 0 or t == n)
    return t


def _ffn_call(x, w, layer, pre, alpha, with_proj):
    n, d = x.shape
    tm = _row_tile(n, ATTN_BLOCK)
    gu, down = w[pre + '_gu'], w[pre + '_down']
    ln = 'ln1' if pre == 'ffn1' else 'ln3'
    g, b = w[ln + '_g'], w[ln + '_b']
    row = lambda width: pl.BlockSpec((tm, width), lambda i: (i, 0))
    in_specs = [row(d), _resident(gu.shape, layer), _resident(down.shape, layer),
                _resident(g.shape, layer), _resident(b.shape, layer)]
    args = [x, gu, down, g, b]
    params = pltpu.CompilerParams(dimension_semantics=("parallel",), vmem_limit_bytes=VMEM_LIMIT)
    if not with_proj:
        return pl.pallas_call(
            functools.partial(_ffn_kernel, alpha=alpha),
            grid=(n // tm,), in_specs=in_specs, out_specs=row(d),
            out_shape=jax.ShapeDtypeStruct((n, d), F32), compiler_params=params,
            name=pre + '_ln')(*args)
    wp = w['w_proj']
    in_specs.append(_resident(wp.shape, layer))
    args.append(wp)
    assert tm % PAGE == 0
    pages = (pl.BlockSpec((tm // PAGE, ATTN_WIDTH, PAGE), lambda i: (i, 0, 0)),
             jax.ShapeDtypeStruct((n // PAGE, ATTN_WIDTH, PAGE), F32))
    rows = lambda width, dt: (row(width), jax.ShapeDtypeStruct((n, width), dt))
    outs = [rows(d, F32), rows(ATTN_WIDTH, BF16), rows(ATTN_WIDTH, BF16), rows(ATTN_WIDTH, BF16),
            (pl.BlockSpec((ATTN_WIDTH, tm), lambda i: (0, i)), jax.ShapeDtypeStruct((ATTN_WIDTH, n), BF16)),
            pages, pages, rows(POOL_WIDTH, F32), rows(CONV_WIDTH, F32)]
    return pl.pallas_call(
        functools.partial(_ffn_proj_kernel, alpha=alpha),
        grid=(n // tm,), in_specs=in_specs,
        out_specs=[o[0] for o in outs], out_shape=[o[1] for o in outs],
        compiler_params=params, name=pre + '_ln_proj')(*args)


def _attn_prompt_kernel(bias_ref, q_ref, k_ref, vt_ref, o_ref, acc_ref, *, blk, n_tiles):
    group = pl.program_id(1)
    qi = pl.program_id(2)
    hpt = HEADS_PER_LANE_TILE
    wide = hpt * blk
    groups = blk // 8
    lane = lax.broadcasted_iota(jnp.int32, (blk, LANES), 1)
    row = lax.broadcasted_iota(jnp.int32, (blk, blk), 0)
    qry = lax.broadcasted_iota(jnp.int32, (blk, blk), 1)
    key = (row & 7) * groups + (row >> 3)
    causal = jnp.concatenate([key < qry] * hpt, axis=1)
    sub = lax.broadcasted_iota(jnp.int32, (8, wide), 0)
    ones_cols = jnp.where(lane < 3, 1.0, 0.0).astype(BF16)
    q_ext = []
    for t in range(n_tiles):
        q = q_ref[:, t * LANES:(t + 1) * LANES].astype(F32)
        rows = []
        for hh in range(hpt):
            b = bias_ref[(group * n_tiles + t) * hpt + hh] * -0.5
            b0 = b.astype(BF16).astype(F32)
            b1 = (b - b0).astype(BF16).astype(F32)
            b2 = (b - b0 - b1).astype(BF16).astype(F32)
            offs = jnp.where(lane == 0, b0, jnp.where(lane == 1, b1, jnp.where(lane == 2, b2, 0.0)))
            rows.append(jnp.concatenate([jnp.where((lane // HEAD_DIM) == hh, q, 0.0), offs], axis=1))
        q_ext.append(jnp.concatenate(rows, axis=0).astype(BF16))

    def scores(chain):
        start, t = chain
        kj = k_ref[pl.ds(start, blk), t * LANES:(t + 1) * LANES]
        return _dot_nt(jnp.concatenate([kj, ones_cols], axis=1), q_ext[t])

    def shift_up(x, k):
        return jnp.where(sub < 8 - k, pltpu.roll(x, 8 - k, axis=0), 1.0)

    def weights(m, carry, masked):
        hth = 0.5 * jnp.tanh(m)
        stay = 0.5 + hth
        beta = 0.5 - hth
        if masked:
            stay = jnp.where(causal, stay, 1.0)
            beta = jnp.where(causal, beta, 0.0)
        after = [None] * groups
        run = None
        for a in range(groups - 1, -1, -1):
            after[a] = run
            x = stay[8 * a:8 * a + 8, :]
            run = x if run is None else run * x
        incl = run
        for k in (1, 2, 4):
            incl = incl * shift_up(incl, k)
        total = incl[0:1, :]
        later = shift_up(incl, 1)
        if carry is not None:
            later = later * carry
            total = total * carry
        wgt = [beta[8 * a:8 * a + 8, :] * (later if after[a] is None else after[a] * later)
               for a in range(groups)]
        return jnp.concatenate(wgt, axis=0).astype(BF16), total

    def values(chain, wgt, init):
        start, t = chain
        pv = _dot(vt_ref[t * LANES:(t + 1) * LANES, pl.ds(start, blk)], wgt)
        acc_ref[t] = pv if init else acc_ref[t] + pv

    def run_blocks(starts, carries):
        first = carries is None
        chains = [(start, t) for start in starts for t in range(n_tiles)]
        carries = [None] * n_tiles if first else list(carries)
        m = [scores(chains[0])]
        for c, (start, t) in enumerate(chains):
            if c + 1 < len(chains):
                m.append(scores(chains[c + 1]))
            wgt, carries[t] = weights(m[c], carries[t], first)
            values(chains[c], wgt, first)
        return tuple(carries)

    def start_of(j):
        return pl.multiple_of(j * blk, blk)

    carries = run_blocks([start_of(qi)], None)
    per = ATTN_BLOCKS_PER_ITER
    carries = lax.fori_loop(
        0, qi // per,
        lambda i, c: run_blocks([start_of(qi - 1 - i * per - r) for r in range(per)], c), carries)
    lax.fori_loop(0, qi % per, lambda i, c: run_blocks([start_of(qi % per - 1 - i)], c), carries)
    for t in range(n_tiles):
        out_t = jnp.concatenate(
            [acc_ref[t, hh * HEAD_DIM:(hh + 1) * HEAD_DIM, hh * blk:(hh + 1) * blk] for hh in range(hpt)], axis=0)
        o_ref[:, t * LANES:(t + 1) * LANES] = out_t.T.astype(o_ref.dtype)


def _attn_prompt_call(qn, kb, vt, bias, batch, seq):
    blk = min(ATTN_BLOCK, seq)
    assert seq % blk == 0 and blk % LANES == 0
    nq = seq // blk
    n_tiles = ATTN_TILES_PER_STEP
    width = n_tiles * LANES
    q_spec = pl.BlockSpec((blk, width), lambda b, g, i: (b * nq + i, g))
    return pl.pallas_call(
        functools.partial(_attn_prompt_kernel, blk=blk, n_tiles=n_tiles),
        grid=(batch, ATTN_WIDTH // width, nq),
        in_specs=[pl.BlockSpec(memory_space=pltpu.SMEM), q_spec,
                  pl.BlockSpec((seq, width), lambda b, g, i: (b, g), pipeline_mode=pl.Buffered(1)),
                  pl.BlockSpec((width, seq), lambda b, g, i: (g, b), pipeline_mode=pl.Buffered(1))],
        out_specs=q_spec,
        out_shape=jax.ShapeDtypeStruct((batch * seq, ATTN_WIDTH), BF16),
        scratch_shapes=[pltpu.VMEM((n_tiles, LANES, HEADS_PER_LANE_TILE * blk), F32)],
        compiler_params=pltpu.CompilerParams(
            dimension_semantics=("parallel", "parallel", "arbitrary"), vmem_limit_bytes=VMEM_LIMIT),
        name='attn_prompt')(bias, qn, kb, vt)


def _attn_sample_kernel(pt_ref, bias_ref, q_ref, *refs, n_pages):
    del pt_ref
    k_refs, v_refs, o_ref = refs[:n_pages], refs[n_pages:2 * n_pages], refs[2 * n_pages]
    page = k_refs[0].shape[1]
    row = lax.broadcasted_iota(jnp.int32, (SAMPLE_ROWS, ATTN_WIDTH), 0)
    lane = lax.broadcasted_iota(jnp.int32, (SAMPLE_ROWS, ATTN_WIDTH), 1)
    own = (lane // HEAD_DIM) == row
    q = jnp.broadcast_to(q_ref[...].astype(F32), (SAMPLE_ROWS, ATTN_WIDTH))
    q_rows = jnp.where(own, q, 0.0).astype(BF16)
    brow = lax.broadcasted_iota(jnp.int32, (SAMPLE_ROWS, page), 0)
    bias = jnp.zeros((SAMPLE_ROWS, page), F32)
    for h in range(N_HEADS):
        bias = jnp.where(brow == h, bias_ref[h] * LOG2E, bias)
    z2 = jnp.concatenate([_dot(q_rows, k_refs[p][...].astype(BF16)) + bias
                          for p in range(n_pages)], axis=0)
    su = _dot(_softplus2(z2).astype(BF16), _suffix_matrix(page))
    within, total = su[:, :page], su[:, page:]
    carries = [jnp.zeros((SAMPLE_ROWS, page), F32)]
    for p in range(1, n_pages):
        carries.append(carries[-1] + total[(p - 1) * SAMPLE_ROWS:p * SAMPLE_ROWS])
    wgt = jnp.exp2(z2 - (within + jnp.concatenate(carries, axis=0))).astype(BF16)
    acc = None
    for p in range(n_pages):
        pv = _dot_nt(wgt[p * SAMPLE_ROWS:(p + 1) * SAMPLE_ROWS], v_refs[p][...].astype(BF16))
        acc = pv if acc is None else acc + pv
    out = jnp.sum(jnp.where(own, acc, 0.0), axis=0, keepdims=True)
    o_ref[...] = out.astype(o_ref.dtype)


def _attn_sample_call(qs, cache_k, cache_v, page_table, bias, layer):
    n_seq, n_pages = page_table.shape
    page = cache_k.shape[3]
    assert page == LANES
    q3 = qs.reshape(n_seq, 1, ATTN_WIDTH)
    row_spec = pl.BlockSpec((None, 1, ATTN_WIDTH), lambda s, pt: (s, 0, 0))

    def page_spec(slot):
        return pl.BlockSpec((None, None, ATTN_WIDTH, page),
                            lambda s, pt: (layer, pt[s * n_pages + slot], 0, 0))

    slots = [n_pages - 1 - p for p in range(n_pages)]
    grid_spec = pltpu.PrefetchScalarGridSpec(
        num_scalar_prefetch=1, grid=(n_seq,),
        in_specs=[pl.BlockSpec(memory_space=pltpu.SMEM), row_spec]
        + [page_spec(s) for s in slots] * 2,
        out_specs=row_spec)
    out = pl.pallas_call(
        functools.partial(_attn_sample_kernel, n_pages=n_pages),
        grid_spec=grid_spec,
        out_shape=jax.ShapeDtypeStruct((n_seq, 1, ATTN_WIDTH), BF16),
        compiler_params=pltpu.CompilerParams(
            dimension_semantics=("parallel",), vmem_limit_bytes=VMEM_LIMIT),
        name='attn_sample')(page_table.reshape(-1), bias, q3,
                            *([cache_k] * n_pages), *([cache_v] * n_pages))
    return out.reshape(n_seq, ATTN_WIDTH)


def _pool_select(win2, win4, win8, win16, cur, count):
    group = lax.broadcasted_iota(jnp.int32, cur.shape, 1) // POOL_GROUP_DIM
    win = jnp.where(group == 0, win2, jnp.where(group == 1, win4, jnp.where(group == 2, win8, win16)))
    return win / count - cur


def _conv_post(y, cb_ref, cg_ref, cbeta_ref):
    y = _layer_norm(y + cb_ref[...], cg_ref[...], cbeta_ref[...])
    return y * jax.nn.sigmoid(y)


def _merge_out(x1, attn, pooled, conv_act, wao_ref, wpool_ref, ps_ref, wco_ref, wg_ref, wo_ref,
               g_ref, b_ref, alpha):
    d = x1.shape[1]
    x1b = x1.astype(BF16)
    group = lax.broadcasted_iota(jnp.int32, pooled.shape, 1) // POOL_GROUP_DIM
    wpool = wpool_ref[...]
    br_p = jnp.concatenate(
        [_dot(jnp.where(group == g, pooled, 0.0).astype(BF16), wpool)
         for g in range(len(POOL_WINDOWS))], axis=1) * ps_ref[...]
    merged = jax.nn.sigmoid(_dot(x1b, wg_ref[:, :d])) * _dot(attn, wao_ref[...])
    merged = merged + jax.nn.sigmoid(_dot(x1b, wg_ref[:, d:2 * d])) * br_p
    merged = merged + jax.nn.sigmoid(_dot(x1b, wg_ref[:, 2 * d:])) * _dot(conv_act.astype(BF16), wco_ref[...])
    m = _dot(merged.astype(BF16), wo_ref[...])
    return _layer_norm(alpha * x1 + m, g_ref[...], b_ref[...])


def _mix_prompt_kernel(x1_ref, attn_ref, u_ref, uh_ref, c_ref, ch_ref, cw_ref, cb_ref, cg_ref, cbeta_ref,
                       wao_ref, wpool_ref, ps_ref, wco_ref, wg_ref, wo_ref, g_ref, b_ref,
                       o_ref, uext_ref, cext_ref, ushift_ref, cshift_ref, *, alpha, tiles_per_seq):
    tm = x1_ref.shape[0]
    tile = pl.program_id(0) % tiles_per_seq
    keep = tile > 0
    uext_ref[0:HALO_ROWS, :] = jnp.where(keep, uh_ref[...], 0.0)
    uext_ref[HALO_ROWS:, :] = u_ref[...]
    cext_ref[0:HALO_ROWS, :] = jnp.where(keep, ch_ref[...], 0.0)
    cext_ref[HALO_ROWS:, :] = c_ref[...]

    def shifted(ext_ref, copies_ref):
        for b in range(8):
            copies_ref[b] = ext_ref[8 - b:HALO_ROWS + tm - b, :]
        return copies_ref

    def back(copies_ref, k):
        a, b = divmod(k, 8)
        off = HALO_ROWS - 8 - 8 * a
        return copies_ref[b, off:off + tm, :]

    cur = u_ref[...]
    u_back = shifted(uext_ref, ushift_ref)
    wins = {}
    win = cur
    for k in range(1, max(POOL_WINDOWS)):
        win = win + back(u_back, k)
        if k + 1 in POOL_WINDOWS:
            wins[k + 1] = win
    pos = tile * tm + lax.broadcasted_iota(jnp.int32, cur.shape, 0)
    width = jnp.left_shift(2, lax.broadcasted_iota(jnp.int32, cur.shape, 1) // POOL_GROUP_DIM)
    count = jnp.minimum(pos + 1, width).astype(F32)
    pooled = _pool_select(wins[2], wins[4], wins[8], wins[16], cur, count)

    c_back = shifted(cext_ref, cshift_ref)
    y = None
    for j in range(CONV_SIZE):
        term = back(c_back, CONV_PAD - j) * cw_ref[j:j + 1, :]
        y = term if y is None else y + term
    conv_act = _conv_post(y, cb_ref, cg_ref, cbeta_ref)

    o_ref[...] = _merge_out(x1_ref[...], attn_ref[...], pooled, conv_act, wao_ref, wpool_ref, ps_ref,
                            wco_ref, wg_ref, wo_ref, g_ref, b_ref, alpha)


def _mix_sample_kernel(x1_ref, attn_ref, u_ref, sp_ref, c_ref, sc_ref, cw_ref, cb_ref, cg_ref, cbeta_ref,
                       wao_ref, wpool_ref, ps_ref, wco_ref, wg_ref, wo_ref, g_ref, b_ref,
                       o_ref, *, alpha, past):
    cur = u_ref[...]
    wins = {}
    win = cur
    for k in range(1, max(POOL_WINDOWS)):
        win = win + sp_ref[POOL_PAD - k]
        if k + 1 in POOL_WINDOWS:
            wins[k + 1] = win
    width = jnp.left_shift(2, lax.broadcasted_iota(jnp.int32, cur.shape, 1) // POOL_GROUP_DIM)
    count = jnp.minimum(past + 1, width).astype(F32)
    pooled = _pool_select(wins[2], wins[4], wins[8], wins[16], cur, count)

    y = c_ref[...] * cw_ref[CONV_PAD:CONV_SIZE, :]
    for j in range(CONV_PAD):
        y = y + sc_ref[j] * cw_ref[j:j + 1, :]
    conv_act = _conv_post(y, cb_ref, cg_ref, cbeta_ref)

    o_ref[...] = _merge_out(x1_ref[...], attn_ref[...], pooled, conv_act, wao_ref, wpool_ref, ps_ref,
                            wco_ref, wg_ref, wo_ref, g_ref, b_ref, alpha)


_MIX_WEIGHTS = ('conv_w', 'conv_b', 'conv_ln_g', 'conv_ln_b', 'w_attn_out', 'w_pool', 'pool_scale',
                'w_conv_out', 'w_gates', 'w_o', 'ln2_g', 'ln2_b')


def _mix_prompt_call(x1, attn, u, glu, w, layer, alpha, seq):
    n, d = x1.shape
    tm = _row_tile(seq, 256)
    assert tm % HALO_ROWS == 0 and HALO_ROWS >= CONV_PAD
    row = lambda width: pl.BlockSpec((tm, width), lambda i: (i, 0))
    halo = lambda width: pl.BlockSpec(
        (HALO_ROWS, width), lambda i: (jnp.maximum(i * (tm // HALO_ROWS) - 1, 0), 0))
    weights = [w[name] for name in _MIX_WEIGHTS]
    return pl.pallas_call(
        functools.partial(_mix_prompt_kernel, alpha=alpha, tiles_per_seq=seq // tm),
        grid=(n // tm,),
        in_specs=[row(d), row(ATTN_WIDTH), row(POOL_WIDTH), halo(POOL_WIDTH), row(CONV_WIDTH), halo(CONV_WIDTH)]
        + [_resident(x.shape, layer) for x in weights],
        out_specs=row(d),
        out_shape=jax.ShapeDtypeStruct((n, d), F32),
        scratch_shapes=[pltpu.VMEM((HALO_ROWS + tm, POOL_WIDTH), F32),
                        pltpu.VMEM((HALO_ROWS + tm, CONV_WIDTH), F32),
                        pltpu.VMEM((8, HALO_ROWS + tm - 8, POOL_WIDTH), F32),
                        pltpu.VMEM((8, HALO_ROWS + tm - 8, CONV_WIDTH), F32)],
        compiler_params=pltpu.CompilerParams(dimension_semantics=("parallel",), vmem_limit_bytes=VMEM_LIMIT),
        name='mix_prompt')(x1, attn, u, u, glu, glu, *weights)


def _mix_sample_call(x1, attn, u, state_pool_t, glu, state_conv_t, w, layer, alpha, past):
    n, d = x1.shape
    full = lambda width: pl.BlockSpec((n, width), lambda i: (0, 0))
    state = lambda x: pl.BlockSpec((None,) + x.shape[1:], lambda i: (layer, 0, 0, 0))
    weights = [w[name] for name in _MIX_WEIGHTS]
    return pl.pallas_call(
        functools.partial(_mix_sample_kernel, alpha=alpha, past=past),
        grid=(1,),
        in_specs=[full(d), full(ATTN_WIDTH), full(POOL_WIDTH), state(state_pool_t),
                  full(CONV_WIDTH), state(state_conv_t)]
        + [_resident(x.shape, layer) for x in weights],
        out_specs=full(d),
        out_shape=jax.ShapeDtypeStruct((n, d), F32),
        compiler_params=pltpu.CompilerParams(dimension_semantics=("arbitrary",), vmem_limit_bytes=VMEM_LIMIT),
        name='mix_sample')(x1, attn, u, state_pool_t, glu, state_conv_t, *weights)


def kernel(x_prompt, x_sample, cache_k, cache_v, state_pool, state_conv, page_table, ln1_g, ln1_b, ffn1_gu, ffn1_down, w_in, sb_bias, w_attn_out, w_pool, pool_scale, conv_w, conv_b, conv_ln_g, conv_ln_b, w_conv_out, w_o, ln2_g, ln2_b, ffn2_gu, ffn2_down, ln3_g, ln3_b):
    depth, d_model = ln1_g.shape
    batch, seq, _ = x_prompt.shape
    n_seq, dec_seq, _ = x_sample.shape
    assert dec_seq == 1 and n_seq == cache_k.shape[2] == PAGE
    n_pages = page_table.shape[1]
    n_phys, page = cache_k.shape[1], cache_k.shape[2]
    past = n_pages * page
    alpha = (2 * depth) ** 0.25
    vec = lambda a: a.reshape(depth, 1, -1)
    w = dict(
        ffn1_gu=ffn1_gu.astype(BF16), ffn1_down=ffn1_down.astype(BF16),
        ffn2_gu=ffn2_gu.astype(BF16), ffn2_down=ffn2_down.astype(BF16),
        w_proj=w_in[:, :, :C_END].astype(BF16), w_gates=w_in[:, :, C_END:].astype(BF16),
        w_attn_out=w_attn_out.astype(BF16), w_conv_out=w_conv_out.astype(BF16), w_o=w_o.astype(BF16),
        w_pool=w_pool.reshape(depth, POOL_WIDTH, -1).astype(BF16),
        conv_w=conv_w.reshape(depth, CONV_SIZE, CONV_WIDTH),
        ln1_g=vec(ln1_g), ln1_b=vec(ln1_b), ln2_g=vec(ln2_g), ln2_b=vec(ln2_b),
        ln3_g=vec(ln3_g), ln3_b=vec(ln3_b), pool_scale=vec(pool_scale),
        conv_b=vec(conv_b), conv_ln_g=vec(conv_ln_g), conv_ln_b=vec(conv_ln_b))
    cache_kt = cache_k.transpose(0, 1, 3, 4, 2).reshape(depth, n_phys, ATTN_WIDTH, page)
    cache_vt = cache_v.transpose(0, 1, 3, 4, 2).reshape(depth, n_phys, ATTN_WIDTH, page)
    state_pool_t = state_pool.transpose(0, 2, 1, 3)
    state_conv_t = state_conv.transpose(0, 2, 1, 3)

    def pages_out(x_t, lead):
        x = x_t.reshape(*lead, N_HEADS, HEAD_DIM, page)
        return jnp.moveaxis(x, -1, -3)

    yp = x_prompt.reshape(batch * seq, d_model)
    ys = x_sample.reshape(n_seq, d_model)
    outs = [[] for _ in range(8)]
    for l in range(depth):
        x1, _, qn, kb, vb, kt, vt, u, glu = _ffn_call(yp, w, l, 'ffn1', alpha, True)
        attn = _attn_prompt_call(qn, kb, vb, sb_bias[l], batch, seq)
        x2 = _mix_prompt_call(x1, attn, u, glu, w, l, alpha, seq)
        yp = _ffn_call(x2, w, l, 'ffn2', alpha, False)
        outs[0].append(pages_out(kt, (batch, seq // page)))
        outs[1].append(pages_out(vt, (batch, seq // page)))
        outs[2].append(u.reshape(batch, seq, POOL_WIDTH)[:, seq - POOL_PAD:])
        outs[3].append(glu.reshape(batch, seq, CONV_WIDTH)[:, seq - CONV_PAD:])

        x1, qs, _, _, _, kt, vt, u, glu = _ffn_call(ys, w, l, 'ffn1', alpha, True)
        attn = _attn_sample_call(qs, cache_kt, cache_vt, page_table, sb_bias[l], l)
        x2 = _mix_sample_call(x1, attn, u, state_pool_t, glu, state_conv_t, w, l, alpha, past)
        ys = _ffn_call(x2, w, l, 'ffn2', alpha, False)
        outs[4].append(pages_out(kt, (1,)).reshape(n_seq, 1, N_HEADS, HEAD_DIM))
        outs[5].append(pages_out(vt, (1,)).reshape(n_seq, 1, N_HEADS, HEAD_DIM))
        outs[6].append(jnp.concatenate([state_pool[l][:, 1:], u[:, None]], axis=1))
        outs[7].append(jnp.concatenate([state_conv[l][:, 1:], glu[:, None]], axis=1))
    stacked = [jnp.stack(o) for o in outs]
    return (yp.reshape(batch, seq, d_model), ys.reshape(n_seq, 1, d_model), *stacked)
```

```python
import functools

import jax
import jax.numpy as jnp
from jax import lax
from jax.experimental import pallas as pl
from jax.experimental.pallas import tpu as pltpu

F32 = jnp.float32
BF16 = jnp.bfloat16

LN_EPS = 1e-5
N_HEADS = 8
HEAD_DIM = 64
ATTN_WIDTH = N_HEADS * HEAD_DIM
POOL_WINDOWS = (2, 4, 8, 16)
POOL_GROUP_DIM = 64
POOL_WIDTH = len(POOL_WINDOWS) * POOL_GROUP_DIM
POOL_PAD = max(POOL_WINDOWS) - 1
CONV_WIDTH = 256
CONV_SIZE = 31
CONV_PAD = CONV_SIZE - 1
N_BRANCHES = 3
Q_END = ATTN_WIDTH
K_END = 2 * ATTN_WIDTH
V_END = 3 * ATTN_WIDTH
U_END = V_END + POOL_WIDTH
CA_END = U_END + CONV_WIDTH
C_END = U_END + 2 * CONV_WIDTH

LANES = 128
PAGE = 128
HEADS_PER_LANE_TILE = LANES // HEAD_DIM
HALO_ROWS = 32
ATTN_BLOCK = 256
ATTN_TILES_PER_STEP = 4
ATTN_BLOCKS_PER_ITER = 4
SAMPLE_ROWS = 16
VMEM_LIMIT = 56 * 1024 * 1024
LOG2E = 1.4426950408889634


def _dot(a, b):
    return jnp.dot(a, b, preferred_element_type=F32)


def _dot_nt(a, b):
    return lax.dot_general(a, b, (((1,), (1,)), ((), ())), preferred_element_type=F32)


def _layer_norm(y, g, b):
    mu = jnp.mean(y, axis=-1, keepdims=True)
    d = y - mu
    var = jnp.mean(d * d, axis=-1, keepdims=True)
    return d * lax.rsqrt(var + LN_EPS) * g + b


def _softplus2(z2):
    return jnp.where(z2 > 64.0, z2, jnp.log2(1.0 + jnp.exp2(z2)))


def _suffix_matrix(n):
    r = lax.broadcasted_iota(jnp.int32, (n, n + LANES), 0)
    c = lax.broadcasted_iota(jnp.int32, (n, n + LANES), 1)
    return jnp.where((r >= c) | (c >= n), 1.0, 0.0).astype(BF16)


def _ff_chunks(d_ff):
    step = 1024
    return [(lo, min(lo + step, d_ff)) for lo in range(0, d_ff, step)]


def _ffn_ln(x, gu_ref, down_ref, g_ref, b_ref, alpha):
    d_ff = down_ref.shape[0]
    xb = x.astype(BF16)
    acc = None
    for lo, hi in _ff_chunks(d_ff):
        a = _dot(xb, gu_ref[:, lo:hi])
        b = _dot(xb, gu_ref[:, d_ff + lo:d_ff + hi])
        h = a * jax.nn.sigmoid(a) * b
        part = _dot(h.astype(BF16), down_ref[lo:hi, :])
        acc = part if acc is None else acc + part
    return _layer_norm(alpha * x + 0.5 * acc, g_ref[...], b_ref[...])


def _ffn_kernel(x_ref, gu_ref, down_ref, g_ref, b_ref, o_ref, *, alpha):
    o_ref[...] = _ffn_ln(x_ref[...], gu_ref, down_ref, g_ref, b_ref, alpha)


def _ffn_proj_kernel(x_ref, gu_ref, down_ref, g_ref, b_ref, wp_ref,
                     x1_ref, qs_ref, qn_ref, kb_ref, vb_ref, kt_ref, vt_ref, u_ref, glu_ref, *, alpha):
    x1 = _ffn_ln(x_ref[...], gu_ref, down_ref, g_ref, b_ref, alpha)
    x1_ref[...] = x1
    h = _dot(x1.astype(BF16), wp_ref[...])
    qs_ref[...] = (h[:, :Q_END] * (LOG2E * HEAD_DIM ** -0.5)).astype(BF16)
    qn_ref[...] = (h[:, :Q_END] * (-0.5 * HEAD_DIM ** -0.5)).astype(BF16)
    k = h[:, Q_END:K_END]
    k_t = k.T
    v_t = h[:, K_END:V_END].T
    tm = k.shape[0]
    r = lax.broadcasted_iota(jnp.int32, (tm, tm), 0)
    c = lax.broadcasted_iota(jnp.int32, (tm, tm), 1)
    perm = jnp.where(c == (r & 7) * (tm // 8) + (r >> 3), 1.0, 0.0).astype(BF16)
    perm_t = jnp.where(r == (c & 7) * (tm // 8) + (c >> 3), 1.0, 0.0).astype(BF16)
    kb_ref[...] = _dot(perm, k.astype(BF16)).astype(BF16)
    vb_ref[...] = _dot(v_t.astype(BF16), perm_t).astype(BF16)
    for p in range(kt_ref.shape[0]):
        kt_ref[p] = k_t[:, p * PAGE:(p + 1) * PAGE]
        vt_ref[p] = v_t[:, p * PAGE:(p + 1) * PAGE]
    u_ref[...] = h[:, V_END:U_END]
    glu_ref[...] = h[:, U_END:CA_END] * jax.nn.sigmoid(h[:, CA_END:C_END])


def _resident(shape, layer):
    nd = len(shape)
    return pl.BlockSpec((None,) + tuple(shape[1:]), lambda *_: (layer,) + (0,) * (nd - 1),
                        pipeline_mode=pl.Buffered(1))


def _row_tile(n, want):
    t = min(n, want)
    assert n % t == 0 and (t % 8 == 0 or t == n)
    return t


def _ffn_call(x, w, layer, pre, alpha, with_proj):
    n, d = x.shape
    tm = _row_tile(n, ATTN_BLOCK)
    gu, down = w[pre + '_gu'], w[pre + '_down']
    ln = 'ln1' if pre == 'ffn1' else 'ln3'
    g, b = w[ln + '_g'], w[ln + '_b']
    row = lambda width: pl.BlockSpec((tm, width), lambda i: (i, 0))
    in_specs = [row(d), _resident(gu.shape, layer), _resident(down.shape, layer),
                _resident(g.shape, layer), _resident(b.shape, layer)]
    args = [x, gu, down, g, b]
    params = pltpu.CompilerParams(dimension_semantics=("parallel",), vmem_limit_bytes=VMEM_LIMIT)
    if not with_proj:
        return pl.pallas_call(
            functools.partial(_ffn_kernel, alpha=alpha),
            grid=(n // tm,), in_specs=in_specs, out_specs=row(d),
            out_shape=jax.ShapeDtypeStruct((n, d), F32), compiler_params=params,
            name=pre + '_ln')(*args)
    wp = w['w_proj']
    in_specs.append(_resident(wp.shape, layer))
    args.append(wp)
    assert tm % PAGE == 0
    pages = (pl.BlockSpec((tm // PAGE, ATTN_WIDTH, PAGE), lambda i: (i, 0, 0)),
             jax.ShapeDtypeStruct((n // PAGE, ATTN_WIDTH, PAGE), F32))
    rows = lambda width, dt: (row(width), jax.ShapeDtypeStruct((n, width), dt))
    outs = [rows(d, F32), rows(ATTN_WIDTH, BF16), rows(ATTN_WIDTH, BF16), rows(ATTN_WIDTH, BF16),
            (pl.BlockSpec((ATTN_WIDTH, tm), lambda i: (0, i)), jax.ShapeDtypeStruct((ATTN_WIDTH, n), BF16)),
            pages, pages, rows(POOL_WIDTH, F32), rows(CONV_WIDTH, F32)]
    return pl.pallas_call(
        functools.partial(_ffn_proj_kernel, alpha=alpha),
        grid=(n // tm,), in_specs=in_specs,
        out_specs=[o[0] for o in outs], out_shape=[o[1] for o in outs],
        compiler_params=params, name=pre + '_ln_proj')(*args)


def _attn_prompt_kernel(bias_ref, q_ref, k_ref, vt_ref, o_ref, acc_ref, *, blk, n_tiles):
    group = pl.program_id(1)
    qi = pl.program_id(2)
    hpt = HEADS_PER_LANE_TILE
    wide = hpt * blk
    groups = blk // 8
    lane = lax.broadcasted_iota(jnp.int32, (blk, LANES), 1)
    row = lax.broadcasted_iota(jnp.int32, (blk, blk), 0)
    qry = lax.broadcasted_iota(jnp.int32, (blk, blk), 1)
    key = (row & 7) * groups + (row >> 3)
    causal = jnp.concatenate([key < qry] * hpt, axis=1)
    sub = lax.broadcasted_iota(jnp.int32, (8, wide), 0)
    ones_cols = jnp.where(lane < 3, 1.0, 0.0).astype(BF16)
    q_ext = []
    for t in range(n_tiles):
        q = q_ref[:, t * LANES:(t + 1) * LANES].astype(F32)
        rows = []
        for hh in range(hpt):
            b = bias_ref[(group * n_tiles + t) * hpt + hh] * -0.5
            b0 = b.astype(BF16).astype(F32)
            b1 = (b - b0).astype(BF16).astype(F32)
            b2 = (b - b0 - b1).astype(BF16).astype(F32)
            offs = jnp.where(lane == 0, b0, jnp.where(lane == 1, b1, jnp.where(lane == 2, b2, 0.0)))
            rows.append(jnp.concatenate([jnp.where((lane // HEAD_DIM) == hh, q, 0.0), offs], axis=1))
        q_ext.append(jnp.concatenate(rows, axis=0).astype(BF16))

    def scores(chain):
        start, t = chain
        kj = k_ref[pl.ds(start, blk), t * LANES:(t + 1) * LANES]
        return _dot_nt(jnp.concatenate([kj, ones_cols], axis=1), q_ext[t])

    def shift_up(x, k):
        return jnp.where(sub < 8 - k, pltpu.roll(x, 8 - k, axis=0), 1.0)

    def scan(run, carry):
        incl = run
        for k in (1, 2, 4):
            incl = incl * shift_up(incl, k)
        total = incl[0:1, :]
        later = shift_up(incl, 1)
        if carry is not None:
            later = later * carry
            total = total * carry
        return later, total

    def values(chain, wgt, init):
        start, t = chain
        pv = _dot(vt_ref[t * LANES:(t + 1) * LANES, pl.ds(start, blk)], jnp.concatenate(wgt, axis=0).astype(BF16))
        acc_ref[t] = pv if init else acc_ref[t] + pv

    def run_blocks(starts, carries):
        first = carries is None
        chains = [(start, t) for start in starts for t in range(n_tiles)]
        carries = [None] * n_tiles if first else list(carries)
        m = [scores(chains[0])]
        prev = None
        for c, (start, t) in enumerate(chains):
            if c + 1 < len(chains):
                m.append(scores(chains[c + 1]))
            stay, run, wgt = [], None, [None] * groups
            for a in range(groups):
                x = 0.5 + 0.5 * jnp.tanh(m[c][8 * a:8 * a + 8, :])
                if first:
                    x = jnp.where(causal[8 * a:8 * a + 8, :], x, 1.0)
                stay.append(x)
                run = x if run is None else run * x
                if prev is not None:
                    back = groups - 1 - a
                    wgt[back] = (1.0 - prev[1][back]) * prev[2]
                    prev[2] = prev[2] * prev[1][back]
            if prev is not None:
                values(prev[0], wgt, first)
            later, carries[t] = scan(run, carries[t])
            prev = [chains[c], stay, later]
        wgt = [None] * groups
        for back in range(groups - 1, -1, -1):
            wgt[back] = (1.0 - prev[1][back]) * prev[2]
            prev[2] = prev[2] * prev[1][back]
        values(prev[0], wgt, first)
        return tuple(carries)

    def start_of(j):
        return pl.multiple_of(j * blk, blk)

    carries = run_blocks([start_of(qi)], None)
    per = ATTN_BLOCKS_PER_ITER
    carries = lax.fori_loop(
        0, qi // per,
        lambda i, c: run_blocks([start_of(qi - 1 - i * per - r) for r in range(per)], c), carries)
    lax.fori_loop(0, qi % per, lambda i, c: run_blocks([start_of(qi % per - 1 - i)], c), carries)
    for t in range(n_tiles):
        out_t = jnp.concatenate(
            [acc_ref[t, hh * HEAD_DIM:(hh + 1) * HEAD_DIM, hh * blk:(hh + 1) * blk] for hh in range(hpt)], axis=0)
        o_ref[:, t * LANES:(t + 1) * LANES] = out_t.T.astype(o_ref.dtype)


def _attn_prompt_call(qn, kb, vt, bias, batch, seq):
    blk = min(ATTN_BLOCK, seq)
    assert seq % blk == 0 and blk % LANES == 0
    nq = seq // blk
    n_tiles = ATTN_TILES_PER_STEP
    width = n_tiles * LANES
    q_spec = pl.BlockSpec((blk, width), lambda b, g, i: (b * nq + i, g))
    return pl.pallas_call(
        functools.partial(_attn_prompt_kernel, blk=blk, n_tiles=n_tiles),
        grid=(batch, ATTN_WIDTH // width, nq),
        in_specs=[pl.BlockSpec(memory_space=pltpu.SMEM), q_spec,
                  pl.BlockSpec((seq, width), lambda b, g, i: (b, g), pipeline_mode=pl.Buffered(1)),
                  pl.BlockSpec((width, seq), lambda b, g, i: (g, b), pipeline_mode=pl.Buffered(1))],
        out_specs=q_spec,
        out_shape=jax.ShapeDtypeStruct((batch * seq, ATTN_WIDTH), BF16),
        scratch_shapes=[pltpu.VMEM((n_tiles, LANES, HEADS_PER_LANE_TILE * blk), F32)],
        compiler_params=pltpu.CompilerParams(
            dimension_semantics=("parallel", "parallel", "arbitrary"), vmem_limit_bytes=VMEM_LIMIT),
        name='attn_prompt')(bias, qn, kb, vt)


def _attn_sample_kernel(pt_ref, bias_ref, q_ref, *refs, n_pages):
    del pt_ref
    k_refs, v_refs, o_ref = refs[:n_pages], refs[n_pages:2 * n_pages], refs[2 * n_pages]
    page = k_refs[0].shape[1]
    row = lax.broadcasted_iota(jnp.int32, (SAMPLE_ROWS, ATTN_WIDTH), 0)
    lane = lax.broadcasted_iota(jnp.int32, (SAMPLE_ROWS, ATTN_WIDTH), 1)
    own = (lane // HEAD_DIM) == row
    q = jnp.broadcast_to(q_ref[...].astype(F32), (SAMPLE_ROWS, ATTN_WIDTH))
    q_rows = jnp.where(own, q, 0.0).astype(BF16)
    brow = lax.broadcasted_iota(jnp.int32, (SAMPLE_ROWS, page), 0)
    bias = jnp.zeros((SAMPLE_ROWS, page), F32)
    for h in range(N_HEADS):
        bias = jnp.where(brow == h, bias_ref[h] * LOG2E, bias)
    z2 = jnp.concatenate([_dot(q_rows, k_refs[p][...].astype(BF16)) + bias
                          for p in range(n_pages)], axis=0)
    su = _dot(_softplus2(z2).astype(BF16), _suffix_matrix(page))
    within, total = su[:, :page], su[:, page:]
    carries = [jnp.zeros((SAMPLE_ROWS, page), F32)]
    for p in range(1, n_pages):
        carries.append(carries[-1] + total[(p - 1) * SAMPLE_ROWS:p * SAMPLE_ROWS])
    wgt = jnp.exp2(z2 - (within + jnp.concatenate(carries, axis=0))).astype(BF16)
    acc = None
    for p in range(n_pages):
        pv = _dot_nt(wgt[p * SAMPLE_ROWS:(p + 1) * SAMPLE_ROWS], v_refs[p][...].astype(BF16))
        acc = pv if acc is None else acc + pv
    out = jnp.sum(jnp.where(own, acc, 0.0), axis=0, keepdims=True)
    o_ref[...] = out.astype(o_ref.dtype)


def _attn_sample_call(qs, cache_k, cache_v, page_table, bias, layer):
    n_seq, n_pages = page_table.shape
    page = cache_k.shape[3]
    assert page == LANES
    q3 = qs.reshape(n_seq, 1, ATTN_WIDTH)
    row_spec = pl.BlockSpec((None, 1, ATTN_WIDTH), lambda s, pt: (s, 0, 0))

    def page_spec(slot):
        return pl.BlockSpec((None, None, ATTN_WIDTH, page),
                            lambda s, pt: (layer, pt[s * n_pages + slot], 0, 0))

    slots = [n_pages - 1 - p for p in range(n_pages)]
    grid_spec = pltpu.PrefetchScalarGridSpec(
        num_scalar_prefetch=1, grid=(n_seq,),
        in_specs=[pl.BlockSpec(memory_space=pltpu.SMEM), row_spec]
        + [page_spec(s) for s in slots] * 2,
        out_specs=row_spec)
    out = pl.pallas_call(
        functools.partial(_attn_sample_kernel, n_pages=n_pages),
        grid_spec=grid_spec,
        out_shape=jax.ShapeDtypeStruct((n_seq, 1, ATTN_WIDTH), BF16),
        compiler_params=pltpu.CompilerParams(
            dimension_semantics=("parallel",), vmem_limit_bytes=VMEM_LIMIT),
        name='attn_sample')(page_table.reshape(-1), bias, q3,
                            *([cache_k] * n_pages), *([cache_v] * n_pages))
    return out.reshape(n_seq, ATTN_WIDTH)


def _pool_select(win2, win4, win8, win16, cur, count):
    group = lax.broadcasted_iota(jnp.int32, cur.shape, 1) // POOL_GROUP_DIM
    win = jnp.where(group == 0, win2, jnp.where(group == 1, win4, jnp.where(group == 2, win8, win16)))
    return win / count - cur


def _conv_post(y, cb_ref, cg_ref, cbeta_ref):
    y = _layer_norm(y + cb_ref[...], cg_ref[...], cbeta_ref[...])
    return y * jax.nn.sigmoid(y)


def _merge_out(x1, attn, pooled, conv_act, wao_ref, wpool_ref, ps_ref, wco_ref, wg_ref, wo_ref,
               g_ref, b_ref, alpha):
    d = x1.shape[1]
    x1b = x1.astype(BF16)
    group = lax.broadcasted_iota(jnp.int32, pooled.shape, 1) // POOL_GROUP_DIM
    wpool = wpool_ref[...]
    br_p = jnp.concatenate(
        [_dot(jnp.where(group == g, pooled, 0.0).astype(BF16), wpool)
         for g in range(len(POOL_WINDOWS))], axis=1) * ps_ref[...]
    merged = jax.nn.sigmoid(_dot(x1b, wg_ref[:, :d])) * _dot(attn, wao_ref[...])
    merged = merged + jax.nn.sigmoid(_dot(x1b, wg_ref[:, d:2 * d])) * br_p
    merged = merged + jax.nn.sigmoid(_dot(x1b, wg_ref[:, 2 * d:])) * _dot(conv_act.astype(BF16), wco_ref[...])
    m = _dot(merged.astype(BF16), wo_ref[...])
    return _layer_norm(alpha * x1 + m, g_ref[...], b_ref[...])


def _mix_prompt_kernel(x1_ref, attn_ref, u_ref, uh_ref, c_ref, ch_ref, cw_ref, cb_ref, cg_ref, cbeta_ref,
                       wao_ref, wpool_ref, ps_ref, wco_ref, wg_ref, wo_ref, g_ref, b_ref,
                       o_ref, uext_ref, cext_ref, ushift_ref, cshift_ref, *, alpha, tiles_per_seq):
    tm = x1_ref.shape[0]
    tile = pl.program_id(0) % tiles_per_seq
    keep = tile > 0
    uext_ref[0:HALO_ROWS, :] = jnp.where(keep, uh_ref[...], 0.0)
    uext_ref[HALO_ROWS:, :] = u_ref[...]
    cext_ref[0:HALO_ROWS, :] = jnp.where(keep, ch_ref[...], 0.0)
    cext_ref[HALO_ROWS:, :] = c_ref[...]

    def shifted(ext_ref, copies_ref):
        for b in range(8):
            copies_ref[b] = ext_ref[8 - b:HALO_ROWS + tm - b, :]
        return copies_ref

    def back(copies_ref, k):
        a, b = divmod(k, 8)
        off = HALO_ROWS - 8 - 8 * a
        return copies_ref[b, off:off + tm, :]

    cur = u_ref[...]
    u_back = shifted(uext_ref, ushift_ref)
    wins = {}
    win = cur
    for k in range(1, max(POOL_WINDOWS)):
        win = win + back(u_back, k)
        if k + 1 in POOL_WINDOWS:
            wins[k + 1] = win
    pos = tile * tm + lax.broadcasted_iota(jnp.int32, cur.shape, 0)
    width = jnp.left_shift(2, lax.broadcasted_iota(jnp.int32, cur.shape, 1) // POOL_GROUP_DIM)
    count = jnp.minimum(pos + 1, width).astype(F32)
    pooled = _pool_select(wins[2], wins[4], wins[8], wins[16], cur, count)

    c_back = shifted(cext_ref, cshift_ref)
    y = None
    for j in range(CONV_SIZE):
        term = back(c_back, CONV_PAD - j) * cw_ref[j:j + 1, :]
        y = term if y is None else y + term
    conv_act = _conv_post(y, cb_ref, cg_ref, cbeta_ref)

    o_ref[...] = _merge_out(x1_ref[...], attn_ref[...], pooled, conv_act, wao_ref, wpool_ref, ps_ref,
                            wco_ref, wg_ref, wo_ref, g_ref, b_ref, alpha)


def _mix_sample_kernel(x1_ref, attn_ref, u_ref, sp_ref, c_ref, sc_ref, cw_ref, cb_ref, cg_ref, cbeta_ref,
                       wao_ref, wpool_ref, ps_ref, wco_ref, wg_ref, wo_ref, g_ref, b_ref,
                       o_ref, *, alpha, past):
    cur = u_ref[...]
    wins = {}
    win = cur
    for k in range(1, max(POOL_WINDOWS)):
        win = win + sp_ref[POOL_PAD - k]
        if k + 1 in POOL_WINDOWS:
            wins[k + 1] = win
    width = jnp.left_shift(2, lax.broadcasted_iota(jnp.int32, cur.shape, 1) // POOL_GROUP_DIM)
    count = jnp.minimum(past + 1, width).astype(F32)
    pooled = _pool_select(wins[2], wins[4], wins[8], wins[16], cur, count)

    y = c_ref[...] * cw_ref[CONV_PAD:CONV_SIZE, :]
    for j in range(CONV_PAD):
        y = y + sc_ref[j] * cw_ref[j:j + 1, :]
    conv_act = _conv_post(y, cb_ref, cg_ref, cbeta_ref)

    o_ref[...] = _merge_out(x1_ref[...], attn_ref[...], pooled, conv_act, wao_ref, wpool_ref, ps_ref,
                            wco_ref, wg_ref, wo_ref, g_ref, b_ref, alpha)


_MIX_WEIGHTS = ('conv_w', 'conv_b', 'conv_ln_g', 'conv_ln_b', 'w_attn_out', 'w_pool', 'pool_scale',
                'w_conv_out', 'w_gates', 'w_o', 'ln2_g', 'ln2_b')


def _mix_prompt_call(x1, attn, u, glu, w, layer, alpha, seq):
    n, d = x1.shape
    tm = _row_tile(seq, 256)
    assert tm % HALO_ROWS == 0 and HALO_ROWS >= CONV_PAD
    row = lambda width: pl.BlockSpec((tm, width), lambda i: (i, 0))
    halo = lambda width: pl.BlockSpec(
        (HALO_ROWS, width), lambda i: (jnp.maximum(i * (tm // HALO_ROWS) - 1, 0), 0))
    weights = [w[name] for name in _MIX_WEIGHTS]
    return pl.pallas_call(
        functools.partial(_mix_prompt_kernel, alpha=alpha, tiles_per_seq=seq // tm),
        grid=(n // tm,),
        in_specs=[row(d), row(ATTN_WIDTH), row(POOL_WIDTH), halo(POOL_WIDTH), row(CONV_WIDTH), halo(CONV_WIDTH)]
        + [_resident(x.shape, layer) for x in weights],
        out_specs=row(d),
        out_shape=jax.ShapeDtypeStruct((n, d), F32),
        scratch_shapes=[pltpu.VMEM((HALO_ROWS + tm, POOL_WIDTH), F32),
                        pltpu.VMEM((HALO_ROWS + tm, CONV_WIDTH), F32),
                        pltpu.VMEM((8, HALO_ROWS + tm - 8, POOL_WIDTH), F32),
                        pltpu.VMEM((8, HALO_ROWS + tm - 8, CONV_WIDTH), F32)],
        compiler_params=pltpu.CompilerParams(dimension_semantics=("parallel",), vmem_limit_bytes=VMEM_LIMIT),
        name='mix_prompt')(x1, attn, u, u, glu, glu, *weights)


def _mix_sample_call(x1, attn, u, state_pool_t, glu, state_conv_t, w, layer, alpha, past):
    n, d = x1.shape
    full = lambda width: pl.BlockSpec((n, width), lambda i: (0, 0))
    state = lambda x: pl.BlockSpec((None,) + x.shape[1:], lambda i: (layer, 0, 0, 0))
    weights = [w[name] for name in _MIX_WEIGHTS]
    return pl.pallas_call(
        functools.partial(_mix_sample_kernel, alpha=alpha, past=past),
        grid=(1,),
        in_specs=[full(d), full(ATTN_WIDTH), full(POOL_WIDTH), state(state_pool_t),
                  full(CONV_WIDTH), state(state_conv_t)]
        + [_resident(x.shape, layer) for x in weights],
        out_specs=full(d),
        out_shape=jax.ShapeDtypeStruct((n, d), F32),
        compiler_params=pltpu.CompilerParams(dimension_semantics=("arbitrary",), vmem_limit_bytes=VMEM_LIMIT),
        name='mix_sample')(x1, attn, u, state_pool_t, glu, state_conv_t, *weights)


def kernel(x_prompt, x_sample, cache_k, cache_v, state_pool, state_conv, page_table, ln1_g, ln1_b, ffn1_gu, ffn1_down, w_in, sb_bias, w_attn_out, w_pool, pool_scale, conv_w, conv_b, conv_ln_g, conv_ln_b, w_conv_out, w_o, ln2_g, ln2_b, ffn2_gu, ffn2_down, ln3_g, ln3_b):
    depth, d_model = ln1_g.shape
    batch, seq, _ = x_prompt.shape
    n_seq, dec_seq, _ = x_sample.shape
    assert dec_seq == 1 and n_seq == cache_k.shape[2] == PAGE
    n_pages = page_table.shape[1]
    n_phys, page = cache_k.shape[1], cache_k.shape[2]
    past = n_pages * page
    alpha = (2 * depth) ** 0.25
    vec = lambda a: a.reshape(depth, 1, -1)
    w = dict(
        ffn1_gu=ffn1_gu.astype(BF16), ffn1_down=ffn1_down.astype(BF16),
        ffn2_gu=ffn2_gu.astype(BF16), ffn2_down=ffn2_down.astype(BF16),
        w_proj=w_in[:, :, :C_END].astype(BF16), w_gates=w_in[:, :, C_END:].astype(BF16),
        w_attn_out=w_attn_out.astype(BF16), w_conv_out=w_conv_out.astype(BF16), w_o=w_o.astype(BF16),
        w_pool=w_pool.reshape(depth, POOL_WIDTH, -1).astype(BF16),
        conv_w=conv_w.reshape(depth, CONV_SIZE, CONV_WIDTH),
        ln1_g=vec(ln1_g), ln1_b=vec(ln1_b), ln2_g=vec(ln2_g), ln2_b=vec(ln2_b),
        ln3_g=vec(ln3_g), ln3_b=vec(ln3_b), pool_scale=vec(pool_scale),
        conv_b=vec(conv_b), conv_ln_g=vec(conv_ln_g), conv_ln_b=vec(conv_ln_b))
    cache_kt = cache_k.transpose(0, 1, 3, 4, 2).reshape(depth, n_phys, ATTN_WIDTH, page)
    cache_vt = cache_v.transpose(0, 1, 3, 4, 2).reshape(depth, n_phys, ATTN_WIDTH, page)
    state_pool_t = state_pool.transpose(0, 2, 1, 3)
    state_conv_t = state_conv.transpose(0, 2, 1, 3)

    def pages_out(x_t, lead):
        x = x_t.reshape(*lead, N_HEADS, HEAD_DIM, page)
        return jnp.moveaxis(x, -1, -3)

    yp = x_prompt.reshape(batch * seq, d_model)
    ys = x_sample.reshape(n_seq, d_model)
    outs = [[] for _ in range(8)]
    for l in range(depth):
        x1, _, qn, kb, vb, kt, vt, u, glu = _ffn_call(yp, w, l, 'ffn1', alpha, True)
        attn = _attn_prompt_call(qn, kb, vb, sb_bias[l], batch, seq)
        x2 = _mix_prompt_call(x1, attn, u, glu, w, l, alpha, seq)
        yp = _ffn_call(x2, w, l, 'ffn2', alpha, False)
        outs[0].append(pages_out(kt, (batch, seq // page)))
        outs[1].append(pages_out(vt, (batch, seq // page)))
        outs[2].append(u.reshape(batch, seq, POOL_WIDTH)[:, seq - POOL_PAD:])
        outs[3].append(glu.reshape(batch, seq, CONV_WIDTH)[:, seq - CONV_PAD:])

        x1, qs, _, _, _, kt, vt, u, glu = _ffn_call(ys, w, l, 'ffn1', alpha, True)
        attn = _attn_sample_call(qs, cache_kt, cache_vt, page_table, sb_bias[l], l)
        x2 = _mix_sample_call(x1, attn, u, state_pool_t, glu, state_conv_t, w, l, alpha, past)
        ys = _ffn_call(x2, w, l, 'ffn2', alpha, False)
        outs[4].append(pages_out(kt, (1,)).reshape(n_seq, 1, N_HEADS, HEAD_DIM))
        outs[5].append(pages_out(vt, (1,)).reshape(n_seq, 1, N_HEADS, HEAD_DIM))
        outs[6].append(jnp.concatenate([state_pool[l][:, 1:], u[:, None]], axis=1))
        outs[7].append(jnp.concatenate([state_conv[l][:, 1:], glu[:, None]], axis=1))
    stacked = [jnp.stack(o) for o in outs]
    return (yp.reshape(batch, seq, d_model), ys.reshape(n_seq, 1, d_model), *stacked)
```

```python
import functools

import jax
import jax.numpy as jnp
from jax import lax
from jax.experimental import pallas as pl
from jax.experimental.pallas import tpu as pltpu

F32 = jnp.float32
BF16 = jnp.bfloat16

LN_EPS = 1e-5
N_HEADS = 8
HEAD_DIM = 64
ATTN_WIDTH = N_HEADS * HEAD_DIM
POOL_WINDOWS = (2, 4, 8, 16)
POOL_GROUP_DIM = 64
POOL_WIDTH = len(POOL_WINDOWS) * POOL_GROUP_DIM
POOL_PAD = max(POOL_WINDOWS) - 1
CONV_WIDTH = 256
CONV_SIZE = 31
CONV_PAD = CONV_SIZE - 1
N_BRANCHES = 3
Q_END = ATTN_WIDTH
K_END = 2 * ATTN_WIDTH
V_END = 3 * ATTN_WIDTH
U_END = V_END + POOL_WIDTH
CA_END = U_END + CONV_WIDTH
C_END = U_END + 2 * CONV_WIDTH

LANES = 128
PAGE = 128
HEADS_PER_LANE_TILE = LANES // HEAD_DIM
HALO_ROWS = 32
ATTN_BLOCK = 256
ATTN_TILES_PER_STEP = 4
ATTN_BLOCKS_PER_ITER = 4
SAMPLE_ROWS = 16
VMEM_LIMIT = 56 * 1024 * 1024
LOG2E = 1.4426950408889634


def _dot(a, b):
    return jnp.dot(a, b, preferred_element_type=F32)


def _dot_nt(a, b):
    return lax.dot_general(a, b, (((1,), (1,)), ((), ())), preferred_element_type=F32)


def _layer_norm(y, g, b):
    mu = jnp.mean(y, axis=-1, keepdims=True)
    d = y - mu
    var = jnp.mean(d * d, axis=-1, keepdims=True)
    return d * lax.rsqrt(var + LN_EPS) * g + b


def _sigmoid(x):
    return 0.5 + 0.5 * jnp.tanh(0.5 * x)


def _softplus2(z2):
    return jnp.where(z2 > 64.0, z2, jnp.log2(1.0 + jnp.exp2(z2)))


def _suffix_matrix(n):
    r = lax.broadcasted_iota(jnp.int32, (n, n + LANES), 0)
    c = lax.broadcasted_iota(jnp.int32, (n, n + LANES), 1)
    return jnp.where((r >= c) | (c >= n), 1.0, 0.0).astype(BF16)


def _ff_chunks(d_ff):
    step = 1024
    return [(lo, min(lo + step, d_ff)) for lo in range(0, d_ff, step)]


def _ffn_ln(x, gu_ref, down_ref, g_ref, b_ref, alpha):
    d_ff = down_ref.shape[0]
    xb = x.astype(BF16)
    acc = None
    for lo, hi in _ff_chunks(d_ff):
        a = _dot(xb, gu_ref[:, lo:hi])
        b = _dot(xb, gu_ref[:, d_ff + lo:d_ff + hi])
        h = a * _sigmoid(a) * b
        part = _dot(h.astype(BF16), down_ref[lo:hi, :])
        acc = part if acc is None else acc + part
    return _layer_norm(alpha * x + 0.5 * acc, g_ref[...], b_ref[...])


def _ffn_kernel(x_ref, gu_ref, down_ref, g_ref, b_ref, o_ref, *, alpha):
    o_ref[...] = _ffn_ln(x_ref[...], gu_ref, down_ref, g_ref, b_ref, alpha)


def _ffn_proj_kernel(x_ref, gu_ref, down_ref, g_ref, b_ref, wp_ref,
                     x1_ref, qs_ref, qn_ref, kb_ref, vb_ref, kt_ref, vt_ref, u_ref, glu_ref, *, alpha):
    x1 = _ffn_ln(x_ref[...], gu_ref, down_ref, g_ref, b_ref, alpha)
    x1_ref[...] = x1
    h = _dot(x1.astype(BF16), wp_ref[...])
    qs_ref[...] = (h[:, :Q_END] * (LOG2E * HEAD_DIM ** -0.5)).astype(BF16)
    qn_ref[...] = (h[:, :Q_END] * (-0.5 * HEAD_DIM ** -0.5)).astype(BF16)
    k = h[:, Q_END:K_END]
    k_t = k.T
    v_t = h[:, K_END:V_END].T
    tm = k.shape[0]
    r = lax.broadcasted_iota(jnp.int32, (tm, tm), 0)
    c = lax.broadcasted_iota(jnp.int32, (tm, tm), 1)
    perm = jnp.where(c == (r & 7) * (tm // 8) + (r >> 3), 1.0, 0.0).astype(BF16)
    scale = lax.bitcast_convert_type((127 - tm // 8 + (c >> 3)) << 23, F32)
    perm_t = jnp.where(r == (c & 7) * (tm // 8) + (c >> 3), scale, 0.0).astype(BF16)
    kb_ref[...] = _dot(perm, k.astype(BF16)).astype(BF16)
    vb_ref[...] = _dot(v_t.astype(BF16), perm_t).astype(BF16)
    for p in range(kt_ref.shape[0]):
        kt_ref[p] = k_t[:, p * PAGE:(p + 1) * PAGE]
        vt_ref[p] = v_t[:, p * PAGE:(p + 1) * PAGE]
    u_ref[...] = h[:, V_END:U_END]
    glu_ref[...] = h[:, U_END:CA_END] * _sigmoid(h[:, CA_END:C_END])


def _resident(shape, layer):
    nd = len(shape)
    return pl.BlockSpec((None,) + tuple(shape[1:]), lambda *_: (layer,) + (0,) * (nd - 1),
                        pipeline_mode=pl.Buffered(1))


def _row_tile(n, want):
    t = min(n, want)
    assert n % t == 0 and (t % 8 == 0 or t == n)
    return t


def _ffn_call(x, w, layer, pre, alpha, with_proj):
    n, d = x.shape
    tm = _row_tile(n, ATTN_BLOCK)
    gu, down = w[pre + '_gu'], w[pre + '_down']
    ln = 'ln1' if pre == 'ffn1' else 'ln3'
    g, b = w[ln + '_g'], w[ln + '_b']
    row = lambda width: pl.BlockSpec((tm, width), lambda i: (i, 0))
    in_specs = [row(d), _resident(gu.shape, layer), _resident(down.shape, layer),
                _resident(g.shape, layer), _resident(b.shape, layer)]
    args = [x, gu, down, g, b]
    params = pltpu.CompilerParams(dimension_semantics=("parallel",), vmem_limit_bytes=VMEM_LIMIT)
    if not with_proj:
        return pl.pallas_call(
            functools.partial(_ffn_kernel, alpha=alpha),
            grid=(n // tm,), in_specs=in_specs, out_specs=row(d),
            out_shape=jax.ShapeDtypeStruct((n, d), F32), compiler_params=params,
            name=pre + '_ln')(*args)
    wp = w['w_proj']
    in_specs.append(_resident(wp.shape, layer))
    args.append(wp)
    assert tm % PAGE == 0
    pages = (pl.BlockSpec((tm // PAGE, ATTN_WIDTH, PAGE), lambda i: (i, 0, 0)),
             jax.ShapeDtypeStruct((n // PAGE, ATTN_WIDTH, PAGE), F32))
    rows = lambda width, dt: (row(width), jax.ShapeDtypeStruct((n, width), dt))
    outs = [rows(d, F32), rows(ATTN_WIDTH, BF16), rows(ATTN_WIDTH, BF16), rows(ATTN_WIDTH, BF16),
            (pl.BlockSpec((ATTN_WIDTH, tm), lambda i: (0, i)), jax.ShapeDtypeStruct((ATTN_WIDTH, n), BF16)),
            pages, pages, rows(POOL_WIDTH, F32), rows(CONV_WIDTH, F32)]
    return pl.pallas_call(
        functools.partial(_ffn_proj_kernel, alpha=alpha),
        grid=(n // tm,), in_specs=in_specs,
        out_specs=[o[0] for o in outs], out_shape=[o[1] for o in outs],
        compiler_params=params, name=pre + '_ln_proj')(*args)


def _attn_prompt_kernel(bias_ref, q_ref, k_ref, vt_ref, o_ref, acc_ref, thheld_ref, laterheld_ref, *, blk, n_tiles):
    group = pl.program_id(1)
    qi = pl.program_id(2)
    hpt = HEADS_PER_LANE_TILE
    wide = hpt * blk
    groups = blk // 8
    lane = lax.broadcasted_iota(jnp.int32, (blk, LANES), 1)
    row = lax.broadcasted_iota(jnp.int32, (blk, blk), 0)
    qry = lax.broadcasted_iota(jnp.int32, (blk, blk), 1)
    key = (row & 7) * groups + (row >> 3)
    causal = jnp.concatenate([key < qry] * hpt, axis=1)
    sub = lax.broadcasted_iota(jnp.int32, (8, wide), 0)
    ones_cols = jnp.where(lane < 3, 1.0, 0.0).astype(BF16)
    q_ext = []
    for t in range(n_tiles):
        q = q_ref[:, t * LANES:(t + 1) * LANES].astype(F32)
        rows = []
        for hh in range(hpt):
            b = bias_ref[(group * n_tiles + t) * hpt + hh] * -0.5
            b0 = b.astype(BF16).astype(F32)
            b1 = (b - b0).astype(BF16).astype(F32)
            b2 = (b - b0 - b1).astype(BF16).astype(F32)
            offs = jnp.where(lane == 0, b0, jnp.where(lane == 1, b1, jnp.where(lane == 2, b2, 0.0)))
            rows.append(jnp.concatenate([jnp.where((lane // HEAD_DIM) == hh, q, 0.0), offs], axis=1))
        q_ext.append(jnp.concatenate(rows, axis=0).astype(BF16))

    def scores(chain):
        start, t = chain
        kj = k_ref[pl.ds(start, blk), t * LANES:(t + 1) * LANES]
        return _dot_nt(jnp.concatenate([kj, ones_cols], axis=1), q_ext[t])

    def shift_up(x, k):
        return jnp.where(sub < 8 - k, pltpu.roll(x, 8 - k, axis=0), 1.0)

    def scan(run, carry):
        incl = run * 2.0 ** -groups
        for k in (1, 2, 4):
            incl = incl * shift_up(incl, k)
        total = incl[0:1, :]
        later = shift_up(incl, 1)
        if carry is not None:
            later = later * carry
            total = total * carry
        return later, total

    def values(chain, wgt, init):
        start, t = chain
        pv = _dot(vt_ref[t * LANES:(t + 1) * LANES, pl.ds(start, blk)], jnp.concatenate(wgt, axis=0).astype(BF16))
        acc_ref[t] = pv if init else acc_ref[t] + pv

    def second_pass_step(pending, wgt, back):
        th = pending[1][back]
        wgt[back] = (1.0 - th) * pending[2]
        pending[2] = pending[2] * (1.0 + th)

    def run_blocks(starts, carries, pending):
        first = carries is None
        chains = [(start, t) for start in starts for t in range(n_tiles)]
        carries = [None] * n_tiles if first else list(carries)
        m = [scores(chains[0])]
        for c, (start, t) in enumerate(chains):
            if c + 1 < len(chains):
                m.append(scores(chains[c + 1]))
            ths, run, wgt = [], None, [None] * groups
            for a in range(groups):
                th = jnp.tanh(m[c][8 * a:8 * a + 8, :])
                if first:
                    th = jnp.where(causal[8 * a:8 * a + 8, :], th, 1.0)
                ths.append(th)
                run = 1.0 + th if run is None else run * (1.0 + th)
                if pending is not None:
                    second_pass_step(pending, wgt, groups - 1 - a)
            if pending is not None:
                values(pending[0], wgt, pending[3])
            later, carries[t] = scan(run, carries[t])
            pending = [chains[c], ths, later, first]
        return tuple(carries), pending

    def finish(pending):
        wgt = [None] * groups
        for back in range(groups - 1, -1, -1):
            second_pass_step(pending, wgt, back)
        values(pending[0], wgt, pending[3])

    def hold(pending):
        for a in range(groups):
            thheld_ref[8 * a:8 * a + 8, :] = pending[1][a]
        laterheld_ref[...] = pending[2]

    def held(start):
        return [(start, n_tiles - 1), [thheld_ref[8 * a:8 * a + 8, :] for a in range(groups)],
                laterheld_ref[...], False]

    def loop_body(starts, carries):
        carries, pending = run_blocks(starts, carries, held(starts[0] + blk))
        hold(pending)
        return carries

    def start_of(j):
        return pl.multiple_of(j * blk, blk)

    carries, pending = run_blocks([start_of(qi)], None, None)
    finish(pending)
    thheld_ref[...] = jnp.ones((blk, wide), F32)
    laterheld_ref[...] = jnp.zeros((8, wide), F32)
    per = ATTN_BLOCKS_PER_ITER
    carries = lax.fori_loop(
        0, qi // per,
        lambda i, c: loop_body([start_of(qi - 1 - i * per - r) for r in range(per)], c), carries)
    lax.fori_loop(0, qi % per, lambda i, c: loop_body([start_of(qi % per - 1 - i)], c), carries)
    finish(held(0))
    for t in range(n_tiles):
        out_t = jnp.concatenate(
            [acc_ref[t, hh * HEAD_DIM:(hh + 1) * HEAD_DIM, hh * blk:(hh + 1) * blk] for hh in range(hpt)], axis=0)
        o_ref[:, t * LANES:(t + 1) * LANES] = out_t.T.astype(o_ref.dtype)


def _attn_prompt_call(qn, kb, vt, bias, batch, seq):
    blk = min(ATTN_BLOCK, seq)
    assert seq % blk == 0 and blk % LANES == 0
    nq = seq // blk
    n_tiles = ATTN_TILES_PER_STEP
    width = n_tiles * LANES
    q_spec = pl.BlockSpec((blk, width), lambda b, g, i: (b * nq + i, g))
    return pl.pallas_call(
        functools.partial(_attn_prompt_kernel, blk=blk, n_tiles=n_tiles),
        grid=(batch, ATTN_WIDTH // width, nq),
        in_specs=[pl.BlockSpec(memory_space=pltpu.SMEM), q_spec,
                  pl.BlockSpec((seq, width), lambda b, g, i: (b, g), pipeline_mode=pl.Buffered(1)),
                  pl.BlockSpec((width, seq), lambda b, g, i: (g, b), pipeline_mode=pl.Buffered(1))],
        out_specs=q_spec,
        out_shape=jax.ShapeDtypeStruct((batch * seq, ATTN_WIDTH), BF16),
        scratch_shapes=[pltpu.VMEM((n_tiles, LANES, HEADS_PER_LANE_TILE * blk), F32),
                        pltpu.VMEM((blk, HEADS_PER_LANE_TILE * blk), F32),
                        pltpu.VMEM((8, HEADS_PER_LANE_TILE * blk), F32)],
        compiler_params=pltpu.CompilerParams(
            dimension_semantics=("parallel", "parallel", "arbitrary"), vmem_limit_bytes=VMEM_LIMIT),
        name='attn_prompt')(bias, qn, kb, vt)


def _attn_sample_kernel(pt_ref, bias_ref, q_ref, *refs, n_pages):
    del pt_ref
    k_refs, v_refs, o_ref = refs[:n_pages], refs[n_pages:2 * n_pages], refs[2 * n_pages]
    page = k_refs[0].shape[1]
    row = lax.broadcasted_iota(jnp.int32, (SAMPLE_ROWS, ATTN_WIDTH), 0)
    lane = lax.broadcasted_iota(jnp.int32, (SAMPLE_ROWS, ATTN_WIDTH), 1)
    own = (lane // HEAD_DIM) == row
    q = jnp.broadcast_to(q_ref[...].astype(F32), (SAMPLE_ROWS, ATTN_WIDTH))
    q_rows = jnp.where(own, q, 0.0).astype(BF16)
    brow = lax.broadcasted_iota(jnp.int32, (SAMPLE_ROWS, page), 0)
    bias = jnp.zeros((SAMPLE_ROWS, page), F32)
    for h in range(N_HEADS):
        bias = jnp.where(brow == h, bias_ref[h] * LOG2E, bias)
    z2 = jnp.concatenate([_dot(q_rows, k_refs[p][...].astype(BF16)) + bias
                          for p in range(n_pages)], axis=0)
    su = _dot(_softplus2(z2).astype(BF16), _suffix_matrix(page))
    within, total = su[:, :page], su[:, page:]
    carries = [jnp.zeros((SAMPLE_ROWS, page), F32)]
    for p in range(1, n_pages):
        carries.append(carries[-1] + total[(p - 1) * SAMPLE_ROWS:p * SAMPLE_ROWS])
    wgt = jnp.exp2(z2 - (within + jnp.concatenate(carries, axis=0))).astype(BF16)
    acc = None
    for p in range(n_pages):
        pv = _dot_nt(wgt[p * SAMPLE_ROWS:(p + 1) * SAMPLE_ROWS], v_refs[p][...].astype(BF16))
        acc = pv if acc is None else acc + pv
    out = jnp.sum(jnp.where(own, acc, 0.0), axis=0, keepdims=True)
    o_ref[...] = out.astype(o_ref.dtype)


def _attn_sample_call(qs, cache_k, cache_v, page_table, bias, layer):
    n_seq, n_pages = page_table.shape
    page = cache_k.shape[3]
    assert page == LANES
    q3 = qs.reshape(n_seq, 1, ATTN_WIDTH)
    row_spec = pl.BlockSpec((None, 1, ATTN_WIDTH), lambda s, pt: (s, 0, 0))

    def page_spec(slot):
        return pl.BlockSpec((None, None, ATTN_WIDTH, page),
                            lambda s, pt: (layer, pt[s * n_pages + slot], 0, 0))

    slots = [n_pages - 1 - p for p in range(n_pages)]
    grid_spec = pltpu.PrefetchScalarGridSpec(
        num_scalar_prefetch=1, grid=(n_seq,),
        in_specs=[pl.BlockSpec(memory_space=pltpu.SMEM), row_spec]
        + [page_spec(s) for s in slots] * 2,
        out_specs=row_spec)
    out = pl.pallas_call(
        functools.partial(_attn_sample_kernel, n_pages=n_pages),
        grid_spec=grid_spec,
        out_shape=jax.ShapeDtypeStruct((n_seq, 1, ATTN_WIDTH), BF16),
        compiler_params=pltpu.CompilerParams(
            dimension_semantics=("parallel",), vmem_limit_bytes=VMEM_LIMIT),
        name='attn_sample')(page_table.reshape(-1), bias, q3,
                            *([cache_k] * n_pages), *([cache_v] * n_pages))
    return out.reshape(n_seq, ATTN_WIDTH)


def _pool_select(win2, win4, win8, win16, cur, count):
    group = lax.broadcasted_iota(jnp.int32, cur.shape, 1) // POOL_GROUP_DIM
    win = jnp.where(group == 0, win2, jnp.where(group == 1, win4, jnp.where(group == 2, win8, win16)))
    return win / count - cur


def _conv_post(y, cb_ref, cg_ref, cbeta_ref):
    y = _layer_norm(y + cb_ref[...], cg_ref[...], cbeta_ref[...])
    return y * _sigmoid(y)


def _merge_out(x1, attn, pooled, conv_act, wao_ref, wpool_ref, ps_ref, wco_ref, wg_ref, wo_ref,
               g_ref, b_ref, alpha):
    d = x1.shape[1]
    x1b = x1.astype(BF16)
    group = lax.broadcasted_iota(jnp.int32, pooled.shape, 1) // POOL_GROUP_DIM
    wpool = wpool_ref[...]
    br_p = jnp.concatenate(
        [_dot(jnp.where(group == g, pooled, 0.0).astype(BF16), wpool)
         for g in range(len(POOL_WINDOWS))], axis=1) * ps_ref[...]
    merged = _sigmoid(_dot(x1b, wg_ref[:, :d])) * _dot(attn, wao_ref[...])
    merged = merged + _sigmoid(_dot(x1b, wg_ref[:, d:2 * d])) * br_p
    merged = merged + _sigmoid(_dot(x1b, wg_ref[:, 2 * d:])) * _dot(conv_act.astype(BF16), wco_ref[...])
    m = _dot(merged.astype(BF16), wo_ref[...])
    return _layer_norm(alpha * x1 + m, g_ref[...], b_ref[...])


def _mix_prompt_kernel(x1_ref, attn_ref, u_ref, uh_ref, c_ref, ch_ref, cw_ref, cb_ref, cg_ref, cbeta_ref,
                       wao_ref, wpool_ref, ps_ref, wco_ref, wg_ref, wo_ref, g_ref, b_ref,
                       o_ref, uext_ref, cext_ref, ushift_ref, cshift_ref, *, alpha, tiles_per_seq):
    tm = x1_ref.shape[0]
    tile = pl.program_id(0) % tiles_per_seq
    keep = tile > 0
    uext_ref[0:HALO_ROWS, :] = jnp.where(keep, uh_ref[...], 0.0)
    uext_ref[HALO_ROWS:, :] = u_ref[...]
    cext_ref[0:HALO_ROWS, :] = jnp.where(keep, ch_ref[...], 0.0)
    cext_ref[HALO_ROWS:, :] = c_ref[...]

    def shifted(ext_ref, copies_ref):
        for b in range(8):
            copies_ref[b] = ext_ref[8 - b:HALO_ROWS + tm - b, :]
        return copies_ref

    def back(copies_ref, k):
        a, b = divmod(k, 8)
        off = HALO_ROWS - 8 - 8 * a
        return copies_ref[b, off:off + tm, :]

    cur = u_ref[...]
    u_back = shifted(uext_ref, ushift_ref)
    wins = {}
    win = cur
    for k in range(1, max(POOL_WINDOWS)):
        win = win + back(u_back, k)
        if k + 1 in POOL_WINDOWS:
            wins[k + 1] = win
    pos = tile * tm + lax.broadcasted_iota(jnp.int32, cur.shape, 0)
    width = jnp.left_shift(2, lax.broadcasted_iota(jnp.int32, cur.shape, 1) // POOL_GROUP_DIM)
    count = jnp.minimum(pos + 1, width).astype(F32)
    pooled = _pool_select(wins[2], wins[4], wins[8], wins[16], cur, count)

    c_back = shifted(cext_ref, cshift_ref)
    y = None
    for j in range(CONV_SIZE):
        term = back(c_back, CONV_PAD - j) * cw_ref[j:j + 1, :]
        y = term if y is None else y + term
    conv_act = _conv_post(y, cb_ref, cg_ref, cbeta_ref)

    o_ref[...] = _merge_out(x1_ref[...], attn_ref[...], pooled, conv_act, wao_ref, wpool_ref, ps_ref,
                            wco_ref, wg_ref, wo_ref, g_ref, b_ref, alpha)


def _mix_sample_kernel(x1_ref, attn_ref, u_ref, sp_ref, c_ref, sc_ref, cw_ref, cb_ref, cg_ref, cbeta_ref,
                       wao_ref, wpool_ref, ps_ref, wco_ref, wg_ref, wo_ref, g_ref, b_ref,
                       o_ref, *, alpha, past):
    cur = u_ref[...]
    wins = {}
    win = cur
    for k in range(1, max(POOL_WINDOWS)):
        win = win + sp_ref[POOL_PAD - k]
        if k + 1 in POOL_WINDOWS:
            wins[k + 1] = win
    width = jnp.left_shift(2, lax.broadcasted_iota(jnp.int32, cur.shape, 1) // POOL_GROUP_DIM)
    count = jnp.minimum(past + 1, width).astype(F32)
    pooled = _pool_select(wins[2], wins[4], wins[8], wins[16], cur, count)

    y = c_ref[...] * cw_ref[CONV_PAD:CONV_SIZE, :]
    for j in range(CONV_PAD):
        y = y + sc_ref[j] * cw_ref[j:j + 1, :]
    conv_act = _conv_post(y, cb_ref, cg_ref, cbeta_ref)

    o_ref[...] = _merge_out(x1_ref[...], attn_ref[...], pooled, conv_act, wao_ref, wpool_ref, ps_ref,
                            wco_ref, wg_ref, wo_ref, g_ref, b_ref, alpha)


_MIX_WEIGHTS = ('conv_w', 'conv_b', 'conv_ln_g', 'conv_ln_b', 'w_attn_out', 'w_pool', 'pool_scale',
                'w_conv_out', 'w_gates', 'w_o', 'ln2_g', 'ln2_b')


def _mix_prompt_call(x1, attn, u, glu, w, layer, alpha, seq):
    n, d = x1.shape
    tm = _row_tile(seq, 256)
    assert tm % HALO_ROWS == 0 and HALO_ROWS >= CONV_PAD
    row = lambda width: pl.BlockSpec((tm, width), lambda i: (i, 0))
    halo = lambda width: pl.BlockSpec(
        (HALO_ROWS, width), lambda i: (jnp.maximum(i * (tm // HALO_ROWS) - 1, 0), 0))
    weights = [w[name] for name in _MIX_WEIGHTS]
    return pl.pallas_call(
        functools.partial(_mix_prompt_kernel, alpha=alpha, tiles_per_seq=seq // tm),
        grid=(n // tm,),
        in_specs=[row(d), row(ATTN_WIDTH), row(POOL_WIDTH), halo(POOL_WIDTH), row(CONV_WIDTH), halo(CONV_WIDTH)]
        + [_resident(x.shape, layer) for x in weights],
        out_specs=row(d),
        out_shape=jax.ShapeDtypeStruct((n, d), F32),
        scratch_shapes=[pltpu.VMEM((HALO_ROWS + tm, POOL_WIDTH), F32),
                        pltpu.VMEM((HALO_ROWS + tm, CONV_WIDTH), F32),
                        pltpu.VMEM((8, HALO_ROWS + tm - 8, POOL_WIDTH), F32),
                        pltpu.VMEM((8, HALO_ROWS + tm - 8, CONV_WIDTH), F32)],
        compiler_params=pltpu.CompilerParams(dimension_semantics=("parallel",), vmem_limit_bytes=VMEM_LIMIT),
        name='mix_prompt')(x1, attn, u, u, glu, glu, *weights)


def _mix_sample_call(x1, attn, u, state_pool_t, glu, state_conv_t, w, layer, alpha, past):
    n, d = x1.shape
    full = lambda width: pl.BlockSpec((n, width), lambda i: (0, 0))
    state = lambda x: pl.BlockSpec((None,) + x.shape[1:], lambda i: (layer, 0, 0, 0))
    weights = [w[name] for name in _MIX_WEIGHTS]
    return pl.pallas_call(
        functools.partial(_mix_sample_kernel, alpha=alpha, past=past),
        grid=(1,),
        in_specs=[full(d), full(ATTN_WIDTH), full(POOL_WIDTH), state(state_pool_t),
                  full(CONV_WIDTH), state(state_conv_t)]
        + [_resident(x.shape, layer) for x in weights],
        out_specs=full(d),
        out_shape=jax.ShapeDtypeStruct((n, d), F32),
        compiler_params=pltpu.CompilerParams(dimension_semantics=("arbitrary",), vmem_limit_bytes=VMEM_LIMIT),
        name='mix_sample')(x1, attn, u, state_pool_t, glu, state_conv_t, *weights)


def kernel(x_prompt, x_sample, cache_k, cache_v, state_pool, state_conv, page_table, ln1_g, ln1_b, ffn1_gu, ffn1_down, w_in, sb_bias, w_attn_out, w_pool, pool_scale, conv_w, conv_b, conv_ln_g, conv_ln_b, w_conv_out, w_o, ln2_g, ln2_b, ffn2_gu, ffn2_down, ln3_g, ln3_b):
    depth, d_model = ln1_g.shape
    batch, seq, _ = x_prompt.shape
    n_seq, dec_seq, _ = x_sample.shape
    assert dec_seq == 1 and n_seq == cache_k.shape[2] == PAGE
    n_pages = page_table.shape[1]
    n_phys, page = cache_k.shape[1], cache_k.shape[2]
    past = n_pages * page
    alpha = (2 * depth) ** 0.25
    vec = lambda a: a.reshape(depth, 1, -1)
    w = dict(
        ffn1_gu=ffn1_gu.astype(BF16), ffn1_down=ffn1_down.astype(BF16),
        ffn2_gu=ffn2_gu.astype(BF16), ffn2_down=ffn2_down.astype(BF16),
        w_proj=w_in[:, :, :C_END].astype(BF16), w_gates=w_in[:, :, C_END:].astype(BF16),
        w_attn_out=w_attn_out.astype(BF16), w_conv_out=w_conv_out.astype(BF16), w_o=w_o.astype(BF16),
        w_pool=w_pool.reshape(depth, POOL_WIDTH, -1).astype(BF16),
        conv_w=conv_w.reshape(depth, CONV_SIZE, CONV_WIDTH),
        ln1_g=vec(ln1_g), ln1_b=vec(ln1_b), ln2_g=vec(ln2_g), ln2_b=vec(ln2_b),
        ln3_g=vec(ln3_g), ln3_b=vec(ln3_b), pool_scale=vec(pool_scale),
        conv_b=vec(conv_b), conv_ln_g=vec(conv_ln_g), conv_ln_b=vec(conv_ln_b))
    cache_kt = cache_k.transpose(0, 1, 3, 4, 2).reshape(depth, n_phys, ATTN_WIDTH, page)
    cache_vt = cache_v.transpose(0, 1, 3, 4, 2).reshape(depth, n_phys, ATTN_WIDTH, page)
    state_pool_t = state_pool.transpose(0, 2, 1, 3)
    state_conv_t = state_conv.transpose(0, 2, 1, 3)

    def pages_out(x_t, lead):
        x = x_t.reshape(*lead, N_HEADS, HEAD_DIM, page)
        return jnp.moveaxis(x, -1, -3)

    yp = x_prompt.reshape(batch * seq, d_model)
    ys = x_sample.reshape(n_seq, d_model)
    outs = [[] for _ in range(8)]
    for l in range(depth):
        x1, _, qn, kb, vb, kt, vt, u, glu = _ffn_call(yp, w, l, 'ffn1', alpha, True)
        attn = _attn_prompt_call(qn, kb, vb, sb_bias[l], batch, seq)
        x2 = _mix_prompt_call(x1, attn, u, glu, w, l, alpha, seq)
        yp = _ffn_call(x2, w, l, 'ffn2', alpha, False)
        outs[0].append(pages_out(kt, (batch, seq // page)))
        outs[1].append(pages_out(vt, (batch, seq // page)))
        outs[2].append(u.reshape(batch, seq, POOL_WIDTH)[:, seq - POOL_PAD:])
        outs[3].append(glu.reshape(batch, seq, CONV_WIDTH)[:, seq - CONV_PAD:])

        x1, qs, _, _, _, kt, vt, u, glu = _ffn_call(ys, w, l, 'ffn1', alpha, True)
        attn = _attn_sample_call(qs, cache_kt, cache_vt, page_table, sb_bias[l], l)
        x2 = _mix_sample_call(x1, attn, u, state_pool_t, glu, state_conv_t, w, l, alpha, past)
        ys = _ffn_call(x2, w, l, 'ffn2', alpha, False)
        outs[4].append(pages_out(kt, (1,)).reshape(n_seq, 1, N_HEADS, HEAD_DIM))
        outs[5].append(pages_out(vt, (1,)).reshape(n_seq, 1, N_HEADS, HEAD_DIM))
        outs[6].append(jnp.concatenate([state_pool[l][:, 1:], u[:, None]], axis=1))
        outs[7].append(jnp.concatenate([state_conv[l][:, 1:], glu[:, None]], axis=1))
    stacked = [jnp.stack(o) for o in outs]
    return (yp.reshape(batch, seq, d_model), ys.reshape(n_seq, 1, d_model), *stacked)
```

```python
import functools

import jax
import jax.numpy as jnp
from jax import lax
from jax.experimental import pallas as pl
from jax.experimental.pallas import tpu as pltpu

F32 = jnp.float32
BF16 = jnp.bfloat16

LN_EPS = 1e-5
N_HEADS = 8
HEAD_DIM = 64
ATTN_WIDTH = N_HEADS * HEAD_DIM
POOL_WINDOWS = (2, 4, 8, 16)
POOL_GROUP_DIM = 64
POOL_WIDTH = len(POOL_WINDOWS) * POOL_GROUP_DIM
POOL_PAD = max(POOL_WINDOWS) - 1
CONV_WIDTH = 256
CONV_SIZE = 31
CONV_PAD = CONV_SIZE - 1
N_BRANCHES = 3
Q_END = ATTN_WIDTH
K_END = 2 * ATTN_WIDTH
V_END = 3 * ATTN_WIDTH
U_END = V_END + POOL_WIDTH
CA_END = U_END + CONV_WIDTH
C_END = U_END + 2 * CONV_WIDTH

LANES = 128
PAGE = 128
HEADS_PER_LANE_TILE = LANES // HEAD_DIM
HALO_ROWS = 32
ATTN_BLOCK = 256
ATTN_TILES_PER_STEP = 4
ATTN_BLOCKS_PER_ITER = 4
SAMPLE_ROWS = 16
VMEM_LIMIT = 56 * 1024 * 1024
LOG2E = 1.4426950408889634


def _dot(a, b):
    return jnp.dot(a, b, preferred_element_type=F32)


def _dot_nt(a, b):
    return lax.dot_general(a, b, (((1,), (1,)), ((), ())), preferred_element_type=F32)


def _layer_norm(y, g, b):
    mu = jnp.mean(y, axis=-1, keepdims=True)
    d = y - mu
    var = jnp.mean(d * d, axis=-1, keepdims=True)
    return d * lax.rsqrt(var + LN_EPS) * g + b


def _sigmoid(x):
    return 0.5 + 0.5 * jnp.tanh(0.5 * x)


def _softplus2(z2):
    return jnp.where(z2 > 64.0, z2, jnp.log2(1.0 + jnp.exp2(z2)))


def _suffix_matrix(n):
    r = lax.broadcasted_iota(jnp.int32, (n, n + LANES), 0)
    c = lax.broadcasted_iota(jnp.int32, (n, n + LANES), 1)
    return jnp.where((r >= c) | (c >= n), 1.0, 0.0).astype(BF16)


def _ff_chunks(d_ff):
    step = 1024
    return [(lo, min(lo + step, d_ff)) for lo in range(0, d_ff, step)]


def _ffn_ln(x, gu_ref, down_ref, g_ref, b_ref, alpha):
    d_ff = down_ref.shape[0]
    xb = x.astype(BF16)
    acc = None
    for lo, hi in _ff_chunks(d_ff):
        a = _dot(xb, gu_ref[:, lo:hi])
        b = _dot(xb, gu_ref[:, d_ff + lo:d_ff + hi])
        h = a * _sigmoid(a) * b
        part = _dot(h.astype(BF16), down_ref[lo:hi, :])
        acc = part if acc is None else acc + part
    return _layer_norm(alpha * x + 0.5 * acc, g_ref[...], b_ref[...])


def _ffn_kernel(x_ref, gu_ref, down_ref, g_ref, b_ref, o_ref, *, alpha):
    o_ref[...] = _ffn_ln(x_ref[...], gu_ref, down_ref, g_ref, b_ref, alpha)


def _ffn_proj_kernel(x_ref, gu_ref, down_ref, g_ref, b_ref, wp_ref, *refs, alpha):
    x1_ref, qs_ref, qn_ref, kb_ref, vb_ref, kt_ref, vt_ref, u_ref, glu_ref = refs[-9:]
    x1 = _ffn_ln(x_ref[...], gu_ref, down_ref, g_ref, b_ref, alpha)
    x1_ref[...] = x1
    h = _dot(x1.astype(BF16), wp_ref[...])
    qs_ref[...] = (h[:, :Q_END] * (LOG2E * HEAD_DIM ** -0.5)).astype(BF16)
    qn_ref[...] = (h[:, :Q_END] * (-0.5 * HEAD_DIM ** -0.5)).astype(BF16)
    k = h[:, Q_END:K_END]
    k_t = k.T
    v_t = h[:, K_END:V_END].T
    tm = k.shape[0]
    r = lax.broadcasted_iota(jnp.int32, (tm, tm), 0)
    c = lax.broadcasted_iota(jnp.int32, (tm, tm), 1)
    perm = jnp.where(c == (r & 7) * (tm // 8) + (r >> 3), 1.0, 0.0).astype(BF16)
    scale = lax.bitcast_convert_type((127 - tm // 8 + (c >> 3)) << 23, F32)
    perm_t = jnp.where(r == (c & 7) * (tm // 8) + (c >> 3), scale, 0.0).astype(BF16)
    kb_ref[...] = _dot(perm, k.astype(BF16)).astype(BF16)
    vb_ref[...] = _dot(v_t.astype(BF16), perm_t).astype(BF16)
    for p in range(kt_ref.shape[0]):
        kt_ref[p] = k_t[:, p * PAGE:(p + 1) * PAGE]
        vt_ref[p] = v_t[:, p * PAGE:(p + 1) * PAGE]
    u_ref[...] = h[:, V_END:U_END]
    glu_ref[...] = h[:, U_END:CA_END] * _sigmoid(h[:, CA_END:C_END])


def _resident(shape, layer):
    nd = len(shape)
    return pl.BlockSpec((None,) + tuple(shape[1:]), lambda *_: (layer,) + (0,) * (nd - 1),
                        pipeline_mode=pl.Buffered(1))


def _row_tile(n, want):
    t = min(n, want)
    assert n % t == 0 and (t % 8 == 0 or t == n)
    return t


def _ffn_call(x, w, layer, pre, alpha, with_proj, page_stacks=None):
    n, d = x.shape
    tm = _row_tile(n, ATTN_BLOCK)
    gu, down = w[pre + '_gu'], w[pre + '_down']
    ln = 'ln1' if pre == 'ffn1' else 'ln3'
    g, b = w[ln + '_g'], w[ln + '_b']
    row = lambda width: pl.BlockSpec((tm, width), lambda i: (i, 0))
    in_specs = [row(d), _resident(gu.shape, layer), _resident(down.shape, layer),
                _resident(g.shape, layer), _resident(b.shape, layer)]
    args = [x, gu, down, g, b]
    params = pltpu.CompilerParams(dimension_semantics=("parallel",), vmem_limit_bytes=VMEM_LIMIT)
    if not with_proj:
        return pl.pallas_call(
            functools.partial(_ffn_kernel, alpha=alpha),
            grid=(n // tm,), in_specs=in_specs, out_specs=row(d),
            out_shape=jax.ShapeDtypeStruct((n, d), F32), compiler_params=params,
            name=pre + '_ln')(*args)
    wp = w['w_proj']
    in_specs.append(_resident(wp.shape, layer))
    args.append(wp)
    assert tm % PAGE == 0
    aliases = {}
    if page_stacks is None:
        pages = (pl.BlockSpec((tm // PAGE, ATTN_WIDTH, PAGE), lambda i: (i, 0, 0)),
                 jax.ShapeDtypeStruct((n // PAGE, ATTN_WIDTH, PAGE), F32))
    else:
        depth, kt_stack, vt_stack = page_stacks
        pages = (pl.BlockSpec((None, tm // PAGE, ATTN_WIDTH, PAGE), lambda i: (layer, i, 0, 0)),
                 jax.ShapeDtypeStruct((depth, n // PAGE, ATTN_WIDTH, PAGE), F32))
        if kt_stack is not None:
            aliases = {len(args): 5, len(args) + 1: 6}
            in_specs += [pl.BlockSpec(memory_space=pl.ANY)] * 2
            args += [kt_stack, vt_stack]
    rows = lambda width, dt: (row(width), jax.ShapeDtypeStruct((n, width), dt))
    outs = [rows(d, F32), rows(ATTN_WIDTH, BF16), rows(ATTN_WIDTH, BF16), rows(ATTN_WIDTH, BF16),
            (pl.BlockSpec((ATTN_WIDTH, tm), lambda i: (0, i)), jax.ShapeDtypeStruct((ATTN_WIDTH, n), BF16)),
            pages, pages, rows(POOL_WIDTH, F32), rows(CONV_WIDTH, F32)]
    return pl.pallas_call(
        functools.partial(_ffn_proj_kernel, alpha=alpha),
        grid=(n // tm,), in_specs=in_specs,
        out_specs=[o[0] for o in outs], out_shape=[o[1] for o in outs],
        input_output_aliases=aliases,
        compiler_params=params, name=pre + '_ln_proj')(*args)


def _attn_prompt_kernel(bias_ref, q_ref, k_ref, vt_ref, o_ref, acc_ref, stayheld_ref, laterheld_ref, *, blk, n_tiles):
    group = pl.program_id(1)
    qi = pl.program_id(2)
    hpt = HEADS_PER_LANE_TILE
    wide = hpt * blk
    groups = blk // 8
    lane = lax.broadcasted_iota(jnp.int32, (blk, LANES), 1)
    row = lax.broadcasted_iota(jnp.int32, (blk, blk), 0)
    qry = lax.broadcasted_iota(jnp.int32, (blk, blk), 1)
    key = (row & 7) * groups + (row >> 3)
    causal = jnp.concatenate([key < qry] * hpt, axis=1)
    sub = lax.broadcasted_iota(jnp.int32, (8, wide), 0)
    ones_cols = jnp.where(lane < 3, 1.0, 0.0).astype(BF16)
    q_ext = []
    for t in range(n_tiles):
        q = q_ref[:, t * LANES:(t + 1) * LANES].astype(F32)
        rows = []
        for hh in range(hpt):
            b = bias_ref[(group * n_tiles + t) * hpt + hh] * -0.5
            b0 = b.astype(BF16).astype(F32)
            b1 = (b - b0).astype(BF16).astype(F32)
            b2 = (b - b0 - b1).astype(BF16).astype(F32)
            offs = jnp.where(lane == 0, b0, jnp.where(lane == 1, b1, jnp.where(lane == 2, b2, 0.0)))
            rows.append(jnp.concatenate([jnp.where((lane // HEAD_DIM) == hh, q, 0.0), offs], axis=1))
        q_ext.append(jnp.concatenate(rows, axis=0).astype(BF16))

    def scores(chain):
        start, t = chain
        kj = k_ref[pl.ds(start, blk), t * LANES:(t + 1) * LANES]
        return _dot_nt(jnp.concatenate([kj, ones_cols], axis=1), q_ext[t])

    def shift_up(x, k):
        return jnp.where(sub < 8 - k, pltpu.roll(x, 8 - k, axis=0), 1.0)

    def scan(run, carry):
        incl = run * 2.0 ** -groups
        for k in (1, 2, 4):
            incl = incl * shift_up(incl, k)
        total = incl[0:1, :]
        later = shift_up(incl, 1)
        if carry is not None:
            later = later * carry
            total = total * carry
        return later, total

    def values(chain, wgt, init):
        start, t = chain
        pv = _dot(vt_ref[t * LANES:(t + 1) * LANES, pl.ds(start, blk)], jnp.concatenate(wgt, axis=0).astype(BF16))
        acc_ref[t] = pv if init else acc_ref[t] + pv

    def second_pass_step(pending, wgt, back):
        stay2 = pending[1][back]
        wgt[back] = (2.0 - stay2) * pending[2]
        pending[2] = pending[2] * stay2

    def run_blocks(starts, carries, pending):
        first = carries is None
        chains = [(start, t) for start in starts for t in range(n_tiles)]
        carries = [None] * n_tiles if first else list(carries)
        m = [scores(chains[0])]
        for c, (start, t) in enumerate(chains):
            if c + 1 < len(chains):
                m.append(scores(chains[c + 1]))
            stay2s, run, wgt = [], None, [None] * groups
            for a in range(groups):
                stay2 = 1.0 + jnp.tanh(m[c][8 * a:8 * a + 8, :])
                if first:
                    stay2 = jnp.where(causal[8 * a:8 * a + 8, :], stay2, 2.0)
                stay2s.append(stay2)
                run = stay2 if run is None else run * stay2
                if pending is not None:
                    second_pass_step(pending, wgt, groups - 1 - a)
            if pending is not None:
                values(pending[0], wgt, pending[3])
            later, carries[t] = scan(run, carries[t])
            pending = [chains[c], stay2s, later, first]
        return tuple(carries), pending

    def finish(pending):
        wgt = [None] * groups
        for back in range(groups - 1, -1, -1):
            second_pass_step(pending, wgt, back)
        values(pending[0], wgt, pending[3])

    def hold(pending):
        for a in range(groups):
            stayheld_ref[8 * a:8 * a + 8, :] = pending[1][a]
        laterheld_ref[...] = pending[2]

    def held(start):
        return [(start, n_tiles - 1), [stayheld_ref[8 * a:8 * a + 8, :] for a in range(groups)],
                laterheld_ref[...], False]

    def loop_body(starts, carries):
        carries, pending = run_blocks(starts, carries, held(starts[0] + blk))
        hold(pending)
        return carries

    def start_of(j):
        return pl.multiple_of(j * blk, blk)

    carries, pending = run_blocks([start_of(qi)], None, None)
    finish(pending)
    stayheld_ref[...] = jnp.full((blk, wide), 2.0, F32)
    laterheld_ref[...] = jnp.zeros((8, wide), F32)
    per = ATTN_BLOCKS_PER_ITER
    carries = lax.fori_loop(
        0, qi // per,
        lambda i, c: loop_body([start_of(qi - 1 - i * per - r) for r in range(per)], c), carries)
    lax.fori_loop(0, qi % per, lambda i, c: loop_body([start_of(qi % per - 1 - i)], c), carries)
    finish(held(0))
    for t in range(n_tiles):
        out_t = jnp.concatenate(
            [acc_ref[t, hh * HEAD_DIM:(hh + 1) * HEAD_DIM, hh * blk:(hh + 1) * blk] for hh in range(hpt)], axis=0)
        o_ref[:, t * LANES:(t + 1) * LANES] = out_t.T.astype(o_ref.dtype)


def _attn_prompt_call(qn, kb, vt, bias, batch, seq):
    blk = min(ATTN_BLOCK, seq)
    assert seq % blk == 0 and blk % LANES == 0
    nq = seq // blk
    n_tiles = ATTN_TILES_PER_STEP
    width = n_tiles * LANES
    q_spec = pl.BlockSpec((blk, width), lambda b, g, i: (b * nq + i, g))
    return pl.pallas_call(
        functools.partial(_attn_prompt_kernel, blk=blk, n_tiles=n_tiles),
        grid=(batch, ATTN_WIDTH // width, nq),
        in_specs=[pl.BlockSpec(memory_space=pltpu.SMEM), q_spec,
                  pl.BlockSpec((seq, width), lambda b, g, i: (b, g), pipeline_mode=pl.Buffered(1)),
                  pl.BlockSpec((width, seq), lambda b, g, i: (g, b), pipeline_mode=pl.Buffered(1))],
        out_specs=q_spec,
        out_shape=jax.ShapeDtypeStruct((batch * seq, ATTN_WIDTH), BF16),
        scratch_shapes=[pltpu.VMEM((n_tiles, LANES, HEADS_PER_LANE_TILE * blk), F32),
                        pltpu.VMEM((blk, HEADS_PER_LANE_TILE * blk), F32),
                        pltpu.VMEM((8, HEADS_PER_LANE_TILE * blk), F32)],
        compiler_params=pltpu.CompilerParams(
            dimension_semantics=("parallel", "parallel", "arbitrary"), vmem_limit_bytes=VMEM_LIMIT),
        name='attn_prompt')(bias, qn, kb, vt)


def _attn_sample_kernel(pt_ref, bias_ref, q_ref, *refs, n_pages):
    del pt_ref
    k_refs, v_refs, o_ref = refs[:n_pages], refs[n_pages:2 * n_pages], refs[2 * n_pages]
    page = k_refs[0].shape[1]
    row = lax.broadcasted_iota(jnp.int32, (SAMPLE_ROWS, ATTN_WIDTH), 0)
    lane = lax.broadcasted_iota(jnp.int32, (SAMPLE_ROWS, ATTN_WIDTH), 1)
    own = (lane // HEAD_DIM) == row
    q = jnp.broadcast_to(q_ref[...].astype(F32), (SAMPLE_ROWS, ATTN_WIDTH))
    q_rows = jnp.where(own, q, 0.0).astype(BF16)
    brow = lax.broadcasted_iota(jnp.int32, (SAMPLE_ROWS, page), 0)
    bias = jnp.zeros((SAMPLE_ROWS, page), F32)
    for h in range(N_HEADS):
        bias = jnp.where(brow == h, bias_ref[h] * LOG2E, bias)
    z2 = jnp.concatenate([_dot(q_rows, k_refs[p][...].astype(BF16)) + bias
                          for p in range(n_pages)], axis=0)
    su = _dot(_softplus2(z2).astype(BF16), _suffix_matrix(page))
    within, total = su[:, :page], su[:, page:]
    carries = [jnp.zeros((SAMPLE_ROWS, page), F32)]
    for p in range(1, n_pages):
        carries.append(carries[-1] + total[(p - 1) * SAMPLE_ROWS:p * SAMPLE_ROWS])
    wgt = jnp.exp2(z2 - (within + jnp.concatenate(carries, axis=0))).astype(BF16)
    acc = None
    for p in range(n_pages):
        pv = _dot_nt(wgt[p * SAMPLE_ROWS:(p + 1) * SAMPLE_ROWS], v_refs[p][...].astype(BF16))
        acc = pv if acc is None else acc + pv
    out = jnp.sum(jnp.where(own, acc, 0.0), axis=0, keepdims=True)
    o_ref[...] = out.astype(o_ref.dtype)


def _attn_sample_call(qs, cache_k, cache_v, page_table, bias, layer):
    n_seq, n_pages = page_table.shape
    page = cache_k.shape[3]
    assert page == LANES
    q3 = qs.reshape(n_seq, 1, ATTN_WIDTH)
    row_spec = pl.BlockSpec((None, 1, ATTN_WIDTH), lambda s, pt: (s, 0, 0))

    def page_spec(slot):
        return pl.BlockSpec((None, None, ATTN_WIDTH, page),
                            lambda s, pt: (layer, pt[s * n_pages + slot], 0, 0))

    slots = [n_pages - 1 - p for p in range(n_pages)]
    grid_spec = pltpu.PrefetchScalarGridSpec(
        num_scalar_prefetch=1, grid=(n_seq,),
        in_specs=[pl.BlockSpec(memory_space=pltpu.SMEM), row_spec]
        + [page_spec(s) for s in slots] * 2,
        out_specs=row_spec)
    out = pl.pallas_call(
        functools.partial(_attn_sample_kernel, n_pages=n_pages),
        grid_spec=grid_spec,
        out_shape=jax.ShapeDtypeStruct((n_seq, 1, ATTN_WIDTH), BF16),
        compiler_params=pltpu.CompilerParams(
            dimension_semantics=("parallel",), vmem_limit_bytes=VMEM_LIMIT),
        name='attn_sample')(page_table.reshape(-1), bias, q3,
                            *([cache_k] * n_pages), *([cache_v] * n_pages))
    return out.reshape(n_seq, ATTN_WIDTH)


def _pool_select(win2, win4, win8, win16, cur, count):
    group = lax.broadcasted_iota(jnp.int32, cur.shape, 1) // POOL_GROUP_DIM
    win = jnp.where(group == 0, win2, jnp.where(group == 1, win4, jnp.where(group == 2, win8, win16)))
    return win / count - cur


def _conv_post(y, cb_ref, cg_ref, cbeta_ref):
    y = _layer_norm(y + cb_ref[...], cg_ref[...], cbeta_ref[...])
    return y * jax.nn.sigmoid(y)


def _merge_out(x1, attn, pooled, conv_act, wao_ref, wpool_ref, ps_ref, wco_ref, wg_ref, wo_ref,
               g_ref, b_ref, alpha):
    d = x1.shape[1]
    x1b = x1.astype(BF16)
    group = lax.broadcasted_iota(jnp.int32, pooled.shape, 1) // POOL_GROUP_DIM
    wpool = wpool_ref[...]
    br_p = jnp.concatenate(
        [_dot(jnp.where(group == g, pooled, 0.0).astype(BF16), wpool)
         for g in range(len(POOL_WINDOWS))], axis=1) * ps_ref[...]
    merged = jax.nn.sigmoid(_dot(x1b, wg_ref[:, :d])) * _dot(attn, wao_ref[...])
    merged = merged + jax.nn.sigmoid(_dot(x1b, wg_ref[:, d:2 * d])) * br_p
    merged = merged + jax.nn.sigmoid(_dot(x1b, wg_ref[:, 2 * d:])) * _dot(conv_act.astype(BF16), wco_ref[...])
    m = _dot(merged.astype(BF16), wo_ref[...])
    return _layer_norm(alpha * x1 + m, g_ref[...], b_ref[...])


def _mix_prompt_kernel(x1_ref, attn_ref, u_ref, uh_ref, c_ref, ch_ref, cw_ref, cb_ref, cg_ref, cbeta_ref,
                       wao_ref, wpool_ref, ps_ref, wco_ref, wg_ref, wo_ref, g_ref, b_ref,
                       o_ref, uext_ref, cext_ref, ushift_ref, cshift_ref, *, alpha, tiles_per_seq):
    tm = x1_ref.shape[0]
    tile = pl.program_id(0) % tiles_per_seq
    keep = tile > 0
    uext_ref[0:HALO_ROWS, :] = jnp.where(keep, uh_ref[...], 0.0)
    uext_ref[HALO_ROWS:, :] = u_ref[...]
    cext_ref[0:HALO_ROWS, :] = jnp.where(keep, ch_ref[...], 0.0)
    cext_ref[HALO_ROWS:, :] = c_ref[...]

    def shifted(ext_ref, copies_ref):
        for b in range(8):
            copies_ref[b] = ext_ref[8 - b:HALO_ROWS + tm - b, :]
        return copies_ref

    def back(copies_ref, k):
        a, b = divmod(k, 8)
        off = HALO_ROWS - 8 - 8 * a
        return copies_ref[b, off:off + tm, :]

    cur = u_ref[...]
    u_back = shifted(uext_ref, ushift_ref)
    wins = {}
    win = cur
    for k in range(1, max(POOL_WINDOWS)):
        win = win + back(u_back, k)
        if k + 1 in POOL_WINDOWS:
            wins[k + 1] = win
    pos = tile * tm + lax.broadcasted_iota(jnp.int32, cur.shape, 0)
    width = jnp.left_shift(2, lax.broadcasted_iota(jnp.int32, cur.shape, 1) // POOL_GROUP_DIM)
    count = jnp.minimum(pos + 1, width).astype(F32)
    pooled = _pool_select(wins[2], wins[4], wins[8], wins[16], cur, count)

    c_back = shifted(cext_ref, cshift_ref)
    y = None
    for j in range(CONV_SIZE):
        term = back(c_back, CONV_PAD - j) * cw_ref[j:j + 1, :]
        y = term if y is None else y + term
    conv_act = _conv_post(y, cb_ref, cg_ref, cbeta_ref)

    o_ref[...] = _merge_out(x1_ref[...], attn_ref[...], pooled, conv_act, wao_ref, wpool_ref, ps_ref,
                            wco_ref, wg_ref, wo_ref, g_ref, b_ref, alpha)


def _mix_sample_kernel(x1_ref, attn_ref, u_ref, sp_ref, c_ref, sc_ref, cw_ref, cb_ref, cg_ref, cbeta_ref,
                       wao_ref, wpool_ref, ps_ref, wco_ref, wg_ref, wo_ref, g_ref, b_ref,
                       o_ref, *, alpha, past):
    cur = u_ref[...]
    wins = {}
    win = cur
    for k in range(1, max(POOL_WINDOWS)):
        win = win + sp_ref[POOL_PAD - k]
        if k + 1 in POOL_WINDOWS:
            wins[k + 1] = win
    width = jnp.left_shift(2, lax.broadcasted_iota(jnp.int32, cur.shape, 1) // POOL_GROUP_DIM)
    count = jnp.minimum(past + 1, width).astype(F32)
    pooled = _pool_select(wins[2], wins[4], wins[8], wins[16], cur, count)

    y = c_ref[...] * cw_ref[CONV_PAD:CONV_SIZE, :]
    for j in range(CONV_PAD):
        y = y + sc_ref[j] * cw_ref[j:j + 1, :]
    conv_act = _conv_post(y, cb_ref, cg_ref, cbeta_ref)

    o_ref[...] = _merge_out(x1_ref[...], attn_ref[...], pooled, conv_act, wao_ref, wpool_ref, ps_ref,
                            wco_ref, wg_ref, wo_ref, g_ref, b_ref, alpha)


_MIX_WEIGHTS = ('conv_w', 'conv_b', 'conv_ln_g', 'conv_ln_b', 'w_attn_out', 'w_pool', 'pool_scale',
                'w_conv_out', 'w_gates', 'w_o', 'ln2_g', 'ln2_b')


def _mix_prompt_call(x1, attn, u, glu, w, layer, alpha, seq):
    n, d = x1.shape
    tm = _row_tile(seq, 256)
    assert tm % HALO_ROWS == 0 and HALO_ROWS >= CONV_PAD
    row = lambda width: pl.BlockSpec((tm, width), lambda i: (i, 0))
    halo = lambda width: pl.BlockSpec(
        (HALO_ROWS, width), lambda i: (jnp.maximum(i * (tm // HALO_ROWS) - 1, 0), 0))
    weights = [w[name] for name in _MIX_WEIGHTS]
    return pl.pallas_call(
        functools.partial(_mix_prompt_kernel, alpha=alpha, tiles_per_seq=seq // tm),
        grid=(n // tm,),
        in_specs=[row(d), row(ATTN_WIDTH), row(POOL_WIDTH), halo(POOL_WIDTH), row(CONV_WIDTH), halo(CONV_WIDTH)]
        + [_resident(x.shape, layer) for x in weights],
        out_specs=row(d),
        out_shape=jax.ShapeDtypeStruct((n, d), F32),
        scratch_shapes=[pltpu.VMEM((HALO_ROWS + tm, POOL_WIDTH), F32),
                        pltpu.VMEM((HALO_ROWS + tm, CONV_WIDTH), F32),
                        pltpu.VMEM((8, HALO_ROWS + tm - 8, POOL_WIDTH), F32),
                        pltpu.VMEM((8, HALO_ROWS + tm - 8, CONV_WIDTH), F32)],
        compiler_params=pltpu.CompilerParams(dimension_semantics=("parallel",), vmem_limit_bytes=VMEM_LIMIT),
        name='mix_prompt')(x1, attn, u, u, glu, glu, *weights)


def _mix_sample_call(x1, attn, u, state_pool_t, glu, state_conv_t, w, layer, alpha, past):
    n, d = x1.shape
    full = lambda width: pl.BlockSpec((n, width), lambda i: (0, 0))
    state = lambda x: pl.BlockSpec((None,) + x.shape[1:], lambda i: (layer, 0, 0, 0))
    weights = [w[name] for name in _MIX_WEIGHTS]
    return pl.pallas_call(
        functools.partial(_mix_sample_kernel, alpha=alpha, past=past),
        grid=(1,),
        in_specs=[full(d), full(ATTN_WIDTH), full(POOL_WIDTH), state(state_pool_t),
                  full(CONV_WIDTH), state(state_conv_t)]
        + [_resident(x.shape, layer) for x in weights],
        out_specs=full(d),
        out_shape=jax.ShapeDtypeStruct((n, d), F32),
        compiler_params=pltpu.CompilerParams(dimension_semantics=("arbitrary",), vmem_limit_bytes=VMEM_LIMIT),
        name='mix_sample')(x1, attn, u, state_pool_t, glu, state_conv_t, *weights)


def kernel(x_prompt, x_sample, cache_k, cache_v, state_pool, state_conv, page_table, ln1_g, ln1_b, ffn1_gu, ffn1_down, w_in, sb_bias, w_attn_out, w_pool, pool_scale, conv_w, conv_b, conv_ln_g, conv_ln_b, w_conv_out, w_o, ln2_g, ln2_b, ffn2_gu, ffn2_down, ln3_g, ln3_b):
    depth, d_model = ln1_g.shape
    batch, seq, _ = x_prompt.shape
    n_seq, dec_seq, _ = x_sample.shape
    assert dec_seq == 1 and n_seq == cache_k.shape[2] == PAGE
    n_pages = page_table.shape[1]
    n_phys, page = cache_k.shape[1], cache_k.shape[2]
    past = n_pages * page
    alpha = (2 * depth) ** 0.25
    vec = lambda a: a.reshape(depth, 1, -1)
    w = dict(
        ffn1_gu=ffn1_gu.astype(BF16), ffn1_down=ffn1_down.astype(BF16),
        ffn2_gu=ffn2_gu.astype(BF16), ffn2_down=ffn2_down.astype(BF16),
        w_proj=w_in[:, :, :C_END].astype(BF16), w_gates=w_in[:, :, C_END:].astype(BF16),
        w_attn_out=w_attn_out.astype(BF16), w_conv_out=w_conv_out.astype(BF16), w_o=w_o.astype(BF16),
        w_pool=w_pool.reshape(depth, POOL_WIDTH, -1).astype(BF16),
        conv_w=conv_w.reshape(depth, CONV_SIZE, CONV_WIDTH),
        ln1_g=vec(ln1_g), ln1_b=vec(ln1_b), ln2_g=vec(ln2_g), ln2_b=vec(ln2_b),
        ln3_g=vec(ln3_g), ln3_b=vec(ln3_b), pool_scale=vec(pool_scale),
        conv_b=vec(conv_b), conv_ln_g=vec(conv_ln_g), conv_ln_b=vec(conv_ln_b))
    cache_kt = cache_k.transpose(0, 1, 3, 4, 2).reshape(depth, n_phys, ATTN_WIDTH, page)
    cache_vt = cache_v.transpose(0, 1, 3, 4, 2).reshape(depth, n_phys, ATTN_WIDTH, page)
    state_pool_t = state_pool.transpose(0, 2, 1, 3)
    state_conv_t = state_conv.transpose(0, 2, 1, 3)

    def pages_out(x_t, lead):
        x = x_t.reshape(*lead, N_HEADS, HEAD_DIM, page)
        return jnp.moveaxis(x, -1, -3)

    yp = x_prompt.reshape(batch * seq, d_model)
    ys = x_sample.reshape(n_seq, d_model)
    outs = [[] for _ in range(8)]
    kt_stack = vt_stack = None
    for l in range(depth):
        x1, _, qn, kb, vb, kt_stack, vt_stack, u, glu = _ffn_call(
            yp, w, l, 'ffn1', alpha, True, (depth, kt_stack, vt_stack))
        attn = _attn_prompt_call(qn, kb, vb, sb_bias[l], batch, seq)
        x2 = _mix_prompt_call(x1, attn, u, glu, w, l, alpha, seq)
        yp = _ffn_call(x2, w, l, 'ffn2', alpha, False)
        outs[2].append(u.reshape(batch, seq, POOL_WIDTH)[:, seq - POOL_PAD:])
        outs[3].append(glu.reshape(batch, seq, CONV_WIDTH)[:, seq - CONV_PAD:])

        x1, qs, _, _, _, kt, vt, u, glu = _ffn_call(ys, w, l, 'ffn1', alpha, True)
        attn = _attn_sample_call(qs, cache_kt, cache_vt, page_table, sb_bias[l], l)
        x2 = _mix_sample_call(x1, attn, u, state_pool_t, glu, state_conv_t, w, l, alpha, past)
        ys = _ffn_call(x2, w, l, 'ffn2', alpha, False)
        outs[4].append(pages_out(kt, (1,)).reshape(n_seq, 1, N_HEADS, HEAD_DIM))
        outs[5].append(pages_out(vt, (1,)).reshape(n_seq, 1, N_HEADS, HEAD_DIM))
        outs[6].append(jnp.concatenate([state_pool[l][:, 1:], u[:, None]], axis=1))
        outs[7].append(jnp.concatenate([state_conv[l][:, 1:], glu[:, None]], axis=1))
    stacked = [pages_out(kt_stack, (depth, batch, seq // page)), pages_out(vt_stack, (depth, batch, seq // page))]
    stacked += [jnp.stack(o) for o in outs[2:]]
    return (yp.reshape(batch, seq, d_model), ys.reshape(n_seq, 1, d_model), *stacked)
```

```python
import functools

import jax
import jax.numpy as jnp
from jax import lax
from jax.experimental import pallas as pl
from jax.experimental.pallas import tpu as pltpu

F32 = jnp.float32
BF16 = jnp.bfloat16

LN_EPS = 1e-5
N_HEADS = 8
HEAD_DIM = 64
ATTN_WIDTH = N_HEADS * HEAD_DIM
POOL_WINDOWS = (2, 4, 8, 16)
POOL_GROUP_DIM = 64
POOL_WIDTH = len(POOL_WINDOWS) * POOL_GROUP_DIM
POOL_PAD = max(POOL_WINDOWS) - 1
CONV_WIDTH = 256
CONV_SIZE = 31
CONV_PAD = CONV_SIZE - 1
N_BRANCHES = 3
Q_END = ATTN_WIDTH
K_END = 2 * ATTN_WIDTH
V_END = 3 * ATTN_WIDTH
U_END = V_END + POOL_WIDTH
CA_END = U_END + CONV_WIDTH
C_END = U_END + 2 * CONV_WIDTH

LANES = 128
PAGE = 128
HEADS_PER_LANE_TILE = LANES // HEAD_DIM
HALO_ROWS = 32
ATTN_BLOCK = 256
ATTN_TILES_PER_STEP = 4
ATTN_BLOCKS_PER_ITER = 4
SAMPLE_ROWS = 16
VMEM_LIMIT = 56 * 1024 * 1024
LOG2E = 1.4426950408889634


def _dot(a, b):
    return jnp.dot(a, b, preferred_element_type=F32)


def _dot_nt(a, b):
    return lax.dot_general(a, b, (((1,), (1,)), ((), ())), preferred_element_type=F32)


def _layer_norm(y, g, b):
    mu = jnp.mean(y, axis=-1, keepdims=True)
    d = y - mu
    var = jnp.mean(d * d, axis=-1, keepdims=True)
    return d * lax.rsqrt(var + LN_EPS) * g + b


def _sigmoid(x):
    return 0.5 + 0.5 * jnp.tanh(0.5 * x)


def _softplus2(z2):
    return jnp.where(z2 > 64.0, z2, jnp.log2(1.0 + jnp.exp2(z2)))


def _suffix_matrix(n):
    r = lax.broadcasted_iota(jnp.int32, (n, n + LANES), 0)
    c = lax.broadcasted_iota(jnp.int32, (n, n + LANES), 1)
    return jnp.where((r >= c) | (c >= n), 1.0, 0.0).astype(BF16)


def _ff_chunks(d_ff):
    step = 1024
    return [(lo, min(lo + step, d_ff)) for lo in range(0, d_ff, step)]


def _ffn_ln(x, gu_ref, down_ref, g_ref, b_ref, alpha):
    d_ff = down_ref.shape[0]
    xb = x.astype(BF16)
    acc = None
    for lo, hi in _ff_chunks(d_ff):
        a = _dot(xb, gu_ref[:, lo:hi])
        b = _dot(xb, gu_ref[:, d_ff + lo:d_ff + hi])
        h = a * _sigmoid(a) * b
        part = _dot(h.astype(BF16), down_ref[lo:hi, :])
        acc = part if acc is None else acc + part
    return _layer_norm(alpha * x + 0.5 * acc, g_ref[...], b_ref[...])


def _ffn_kernel(x_ref, gu_ref, down_ref, g_ref, b_ref, o_ref, *, alpha):
    o_ref[...] = _ffn_ln(x_ref[...], gu_ref, down_ref, g_ref, b_ref, alpha)


def _ffn_proj_kernel(x_ref, gu_ref, down_ref, g_ref, b_ref, wp_ref, *refs, alpha):
    x1_ref, qs_ref, qn_ref, kb_ref, vb_ref, kt_ref, vt_ref, u_ref, glu_ref = refs[-9:]
    x1 = _ffn_ln(x_ref[...], gu_ref, down_ref, g_ref, b_ref, alpha)
    x1_ref[...] = x1
    h = _dot(x1.astype(BF16), wp_ref[...])
    qs_ref[...] = (h[:, :Q_END] * (LOG2E * HEAD_DIM ** -0.5)).astype(BF16)
    qn_ref[...] = (h[:, :Q_END] * (-0.5 * HEAD_DIM ** -0.5)).astype(BF16)
    k = h[:, Q_END:K_END]
    k_t = k.T
    v_t = h[:, K_END:V_END].T
    tm = k.shape[0]
    r = lax.broadcasted_iota(jnp.int32, (tm, tm), 0)
    c = lax.broadcasted_iota(jnp.int32, (tm, tm), 1)
    perm = jnp.where(c == (r & 7) * (tm // 8) + (r >> 3), 1.0, 0.0).astype(BF16)
    scale = lax.bitcast_convert_type((127 - tm // 8 + (c >> 3)) << 23, F32)
    perm_t = jnp.where(r == (c & 7) * (tm // 8) + (c >> 3), scale, 0.0).astype(BF16)
    kb_ref[...] = _dot(perm, k.astype(BF16)).astype(BF16)
    vb_ref[...] = _dot(v_t.astype(BF16), perm_t).astype(BF16)
    for p in range(kt_ref.shape[0]):
        kt_ref[p] = k_t[:, p * PAGE:(p + 1) * PAGE]
        vt_ref[p] = v_t[:, p * PAGE:(p + 1) * PAGE]
    u_ref[...] = h[:, V_END:U_END]
    glu_ref[...] = h[:, U_END:CA_END] * _sigmoid(h[:, CA_END:C_END])


def _resident(shape, layer):
    nd = len(shape)
    return pl.BlockSpec((None,) + tuple(shape[1:]), lambda *_: (layer,) + (0,) * (nd - 1),
                        pipeline_mode=pl.Buffered(1))


def _row_tile(n, want):
    t = min(n, want)
    assert n % t == 0 and (t % 8 == 0 or t == n)
    return t


def _ffn_call(x, w, layer, pre, alpha, with_proj, page_stacks=None):
    n, d = x.shape
    tm = _row_tile(n, ATTN_BLOCK)
    gu, down = w[pre + '_gu'], w[pre + '_down']
    ln = 'ln1' if pre == 'ffn1' else 'ln3'
    g, b = w[ln + '_g'], w[ln + '_b']
    row = lambda width: pl.BlockSpec((tm, width), lambda i: (i, 0))
    in_specs = [row(d), _resident(gu.shape, layer), _resident(down.shape, layer),
                _resident(g.shape, layer), _resident(b.shape, layer)]
    args = [x, gu, down, g, b]
    params = pltpu.CompilerParams(dimension_semantics=("parallel",), vmem_limit_bytes=VMEM_LIMIT)
    if not with_proj:
        return pl.pallas_call(
            functools.partial(_ffn_kernel, alpha=alpha),
            grid=(n // tm,), in_specs=in_specs, out_specs=row(d),
            out_shape=jax.ShapeDtypeStruct((n, d), F32), compiler_params=params,
            name=pre + '_ln')(*args)
    wp = w['w_proj']
    in_specs.append(_resident(wp.shape, layer))
    args.append(wp)
    assert tm % PAGE == 0
    aliases = {}
    if page_stacks is None:
        pages = (pl.BlockSpec((tm // PAGE, ATTN_WIDTH, PAGE), lambda i: (i, 0, 0)),
                 jax.ShapeDtypeStruct((n // PAGE, ATTN_WIDTH, PAGE), F32))
    else:
        pages = (pl.BlockSpec((None, tm // PAGE, ATTN_WIDTH, PAGE), lambda i: (layer, i, 0, 0)),
                 jax.ShapeDtypeStruct(page_stacks[0].shape, F32))
        aliases = {len(args): 5, len(args) + 1: 6}
        in_specs += [pl.BlockSpec(memory_space=pl.ANY)] * 2
        args += list(page_stacks)
    rows = lambda width, dt: (row(width), jax.ShapeDtypeStruct((n, width), dt))
    outs = [rows(d, F32), rows(ATTN_WIDTH, BF16), rows(ATTN_WIDTH, BF16), rows(ATTN_WIDTH, BF16),
            (pl.BlockSpec((ATTN_WIDTH, tm), lambda i: (0, i)), jax.ShapeDtypeStruct((ATTN_WIDTH, n), BF16)),
            pages, pages, rows(POOL_WIDTH, F32), rows(CONV_WIDTH, F32)]
    return pl.pallas_call(
        functools.partial(_ffn_proj_kernel, alpha=alpha),
        grid=(n // tm,), in_specs=in_specs,
        out_specs=[o[0] for o in outs], out_shape=[o[1] for o in outs],
        input_output_aliases=aliases,
        compiler_params=params, name=pre + '_ln_proj')(*args)


def _attn_prompt_kernel(bias_ref, q_ref, k_ref, vt_ref, o_ref, acc_ref, stayheld_ref, laterheld_ref, *, blk, n_tiles):
    group = pl.program_id(1)
    qi = pl.program_id(2)
    hpt = HEADS_PER_LANE_TILE
    wide = hpt * blk
    groups = blk // 8
    lane = lax.broadcasted_iota(jnp.int32, (blk, LANES), 1)
    row = lax.broadcasted_iota(jnp.int32, (blk, blk), 0)
    qry = lax.broadcasted_iota(jnp.int32, (blk, blk), 1)
    key = (row & 7) * groups + (row >> 3)
    causal = jnp.concatenate([key < qry] * hpt, axis=1)
    sub = lax.broadcasted_iota(jnp.int32, (8, wide), 0)
    ones_cols = jnp.where(lane < 3, 1.0, 0.0).astype(BF16)
    q_ext = []
    for t in range(n_tiles):
        q = q_ref[:, t * LANES:(t + 1) * LANES].astype(F32)
        rows = []
        for hh in range(hpt):
            b = bias_ref[(group * n_tiles + t) * hpt + hh] * -0.5
            b0 = b.astype(BF16).astype(F32)
            b1 = (b - b0).astype(BF16).astype(F32)
            b2 = (b - b0 - b1).astype(BF16).astype(F32)
            offs = jnp.where(lane == 0, b0, jnp.where(lane == 1, b1, jnp.where(lane == 2, b2, 0.0)))
            rows.append(jnp.concatenate([jnp.where((lane // HEAD_DIM) == hh, q, 0.0), offs], axis=1))
        q_ext.append(jnp.concatenate(rows, axis=0).astype(BF16))

    def scores(chain):
        start, t = chain
        kj = k_ref[pl.ds(start, blk), t * LANES:(t + 1) * LANES]
        return _dot_nt(jnp.concatenate([kj, ones_cols], axis=1), q_ext[t])

    def shift_up(x, k):
        return jnp.where(sub < 8 - k, pltpu.roll(x, 8 - k, axis=0), 1.0)

    def scan(run, carry):
        incl = run * 2.0 ** -groups
        for k in (1, 2, 4):
            incl = incl * shift_up(incl, k)
        total = incl[0:1, :]
        later = shift_up(incl, 1)
        if carry is not None:
            later = later * carry
            total = total * carry
        return later, total

    def values(chain, wgt, init):
        start, t = chain
        pv = _dot(vt_ref[t * LANES:(t + 1) * LANES, pl.ds(start, blk)], jnp.concatenate(wgt, axis=0).astype(BF16))
        acc_ref[t] = pv if init else acc_ref[t] + pv

    def second_pass_step(pending, wgt, back):
        stay2 = pending[1][back]
        wgt[back] = (2.0 - stay2) * pending[2]
        pending[2] = pending[2] * stay2

    def run_blocks(starts, carries, pending):
        first = carries is None
        chains = [(start, t) for start in starts for t in range(n_tiles)]
        carries = [None] * n_tiles if first else list(carries)
        m = [scores(chains[0])]
        for c, (start, t) in enumerate(chains):
            if c + 1 < len(chains):
                m.append(scores(chains[c + 1]))
            stay2s, run, wgt = [], None, [None] * groups
            for a in range(groups):
                stay2 = 1.0 + jnp.tanh(m[c][8 * a:8 * a + 8, :])
                if first:
                    stay2 = jnp.where(causal[8 * a:8 * a + 8, :], stay2, 2.0)
                stay2s.append(stay2)
                run = stay2 if run is None else run * stay2
                if pending is not None:
                    second_pass_step(pending, wgt, groups - 1 - a)
            if pending is not None:
                values(pending[0], wgt, pending[3])
            later, carries[t] = scan(run, carries[t])
            pending = [chains[c], stay2s, later, first]
        return tuple(carries), pending

    def finish(pending):
        wgt = [None] * groups
        for back in range(groups - 1, -1, -1):
            second_pass_step(pending, wgt, back)
        values(pending[0], wgt, pending[3])

    def hold(pending):
        for a in range(groups):
            stayheld_ref[8 * a:8 * a + 8, :] = pending[1][a]
        laterheld_ref[...] = pending[2]

    def held(start):
        return [(start, n_tiles - 1), [stayheld_ref[8 * a:8 * a + 8, :] for a in range(groups)],
                laterheld_ref[...], False]

    def loop_body(starts, carries):
        carries, pending = run_blocks(starts, carries, held(starts[0] + blk))
        hold(pending)
        return carries

    def start_of(j):
        return pl.multiple_of(j * blk, blk)

    carries, pending = run_blocks([start_of(qi)], None, None)
    finish(pending)
    stayheld_ref[...] = jnp.full((blk, wide), 2.0, F32)
    laterheld_ref[...] = jnp.zeros((8, wide), F32)
    per = ATTN_BLOCKS_PER_ITER
    carries = lax.fori_loop(
        0, qi // per,
        lambda i, c: loop_body([start_of(qi - 1 - i * per - r) for r in range(per)], c), carries)
    lax.fori_loop(0, qi % per, lambda i, c: loop_body([start_of(qi % per - 1 - i)], c), carries)
    finish(held(0))
    for t in range(n_tiles):
        out_t = jnp.concatenate(
            [acc_ref[t, hh * HEAD_DIM:(hh + 1) * HEAD_DIM, hh * blk:(hh + 1) * blk] for hh in range(hpt)], axis=0)
        o_ref[:, t * LANES:(t + 1) * LANES] = out_t.T.astype(o_ref.dtype)


def _attn_prompt_call(qn, kb, vt, bias, batch, seq):
    blk = min(ATTN_BLOCK, seq)
    assert seq % blk == 0 and blk % LANES == 0
    nq = seq // blk
    n_tiles = ATTN_TILES_PER_STEP
    width = n_tiles * LANES
    q_spec = pl.BlockSpec((blk, width), lambda b, g, i: (b * nq + i, g))
    return pl.pallas_call(
        functools.partial(_attn_prompt_kernel, blk=blk, n_tiles=n_tiles),
        grid=(batch, ATTN_WIDTH // width, nq),
        in_specs=[pl.BlockSpec(memory_space=pltpu.SMEM), q_spec,
                  pl.BlockSpec((seq, width), lambda b, g, i: (b, g), pipeline_mode=pl.Buffered(1)),
                  pl.BlockSpec((width, seq), lambda b, g, i: (g, b), pipeline_mode=pl.Buffered(1))],
        out_specs=q_spec,
        out_shape=jax.ShapeDtypeStruct((batch * seq, ATTN_WIDTH), BF16),
        scratch_shapes=[pltpu.VMEM((n_tiles, LANES, HEADS_PER_LANE_TILE * blk), F32),
                        pltpu.VMEM((blk, HEADS_PER_LANE_TILE * blk), F32),
                        pltpu.VMEM((8, HEADS_PER_LANE_TILE * blk), F32)],
        compiler_params=pltpu.CompilerParams(
            dimension_semantics=("parallel", "parallel", "arbitrary"), vmem_limit_bytes=VMEM_LIMIT),
        name='attn_prompt')(bias, qn, kb, vt)


def _attn_sample_kernel(pt_ref, bias_ref, q_ref, *refs, n_pages):
    del pt_ref
    k_refs, v_refs, o_ref = refs[:n_pages], refs[n_pages:2 * n_pages], refs[2 * n_pages]
    page = k_refs[0].shape[1]
    row = lax.broadcasted_iota(jnp.int32, (SAMPLE_ROWS, ATTN_WIDTH), 0)
    lane = lax.broadcasted_iota(jnp.int32, (SAMPLE_ROWS, ATTN_WIDTH), 1)
    own = (lane // HEAD_DIM) == row
    q = jnp.broadcast_to(q_ref[...].astype(F32), (SAMPLE_ROWS, ATTN_WIDTH))
    q_rows = jnp.where(own, q, 0.0).astype(BF16)
    brow = lax.broadcasted_iota(jnp.int32, (SAMPLE_ROWS, page), 0)
    bias = jnp.zeros((SAMPLE_ROWS, page), F32)
    for h in range(N_HEADS):
        bias = jnp.where(brow == h, bias_ref[h] * LOG2E, bias)
    z2 = jnp.concatenate([_dot(q_rows, k_refs[p][...].astype(BF16)) + bias
                          for p in range(n_pages)], axis=0)
    su = _dot(_softplus2(z2).astype(BF16), _suffix_matrix(page))
    within, total = su[:, :page], su[:, page:]
    carries = [jnp.zeros((SAMPLE_ROWS, page), F32)]
    for p in range(1, n_pages):
        carries.append(carries[-1] + total[(p - 1) * SAMPLE_ROWS:p * SAMPLE_ROWS])
    wgt = jnp.exp2(z2 - (within + jnp.concatenate(carries, axis=0))).astype(BF16)
    acc = None
    for p in range(n_pages):
        pv = _dot_nt(wgt[p * SAMPLE_ROWS:(p + 1) * SAMPLE_ROWS], v_refs[p][...].astype(BF16))
        acc = pv if acc is None else acc + pv
    out = jnp.sum(jnp.where(own, acc, 0.0), axis=0, keepdims=True)
    o_ref[...] = out.astype(o_ref.dtype)


def _attn_sample_call(qs, cache_k, cache_v, page_table, bias, layer):
    n_seq, n_pages = page_table.shape
    page = cache_k.shape[3]
    assert page == LANES
    q3 = qs.reshape(n_seq, 1, ATTN_WIDTH)
    row_spec = pl.BlockSpec((None, 1, ATTN_WIDTH), lambda s, pt: (s, 0, 0))

    def page_spec(slot):
        return pl.BlockSpec((None, None, ATTN_WIDTH, page),
                            lambda s, pt: (layer, pt[s * n_pages + slot], 0, 0))

    slots = [n_pages - 1 - p for p in range(n_pages)]
    grid_spec = pltpu.PrefetchScalarGridSpec(
        num_scalar_prefetch=1, grid=(n_seq,),
        in_specs=[pl.BlockSpec(memory_space=pltpu.SMEM), row_spec]
        + [page_spec(s) for s in slots] * 2,
        out_specs=row_spec)
    out = pl.pallas_call(
        functools.partial(_attn_sample_kernel, n_pages=n_pages),
        grid_spec=grid_spec,
        out_shape=jax.ShapeDtypeStruct((n_seq, 1, ATTN_WIDTH), BF16),
        compiler_params=pltpu.CompilerParams(
            dimension_semantics=("parallel",), vmem_limit_bytes=VMEM_LIMIT),
        name='attn_sample')(page_table.reshape(-1), bias, q3,
                            *([cache_k] * n_pages), *([cache_v] * n_pages))
    return out.reshape(n_seq, ATTN_WIDTH)


def _pool_select(win2, win4, win8, win16, cur, count):
    group = lax.broadcasted_iota(jnp.int32, cur.shape, 1) // POOL_GROUP_DIM
    win = jnp.where(group == 0, win2, jnp.where(group == 1, win4, jnp.where(group == 2, win8, win16)))
    return win / count - cur


def _conv_post(y, cb_ref, cg_ref, cbeta_ref):
    y = _layer_norm(y + cb_ref[...], cg_ref[...], cbeta_ref[...])
    return y * jax.nn.sigmoid(y)


def _merge_out(x1, attn, pooled, conv_act, wao_ref, wpool_ref, ps_ref, wco_ref, wg_ref, wo_ref,
               g_ref, b_ref, alpha):
    d = x1.shape[1]
    x1b = x1.astype(BF16)
    group = lax.broadcasted_iota(jnp.int32, pooled.shape, 1) // POOL_GROUP_DIM
    wpool = wpool_ref[...]
    br_p = jnp.concatenate(
        [_dot(jnp.where(group == g, pooled, 0.0).astype(BF16), wpool)
         for g in range(len(POOL_WINDOWS))], axis=1) * ps_ref[...]
    merged = jax.nn.sigmoid(_dot(x1b, wg_ref[:, :d])) * _dot(attn, wao_ref[...])
    merged = merged + jax.nn.sigmoid(_dot(x1b, wg_ref[:, d:2 * d])) * br_p
    merged = merged + jax.nn.sigmoid(_dot(x1b, wg_ref[:, 2 * d:])) * _dot(conv_act.astype(BF16), wco_ref[...])
    m = _dot(merged.astype(BF16), wo_ref[...])
    return _layer_norm(alpha * x1 + m, g_ref[...], b_ref[...])


def _mix_prompt_kernel(x1_ref, attn_ref, u_ref, uh_ref, c_ref, ch_ref, cw_ref, cb_ref, cg_ref, cbeta_ref,
                       wao_ref, wpool_ref, ps_ref, wco_ref, wg_ref, wo_ref, g_ref, b_ref,
                       o_ref, uext_ref, cext_ref, ushift_ref, cshift_ref, *, alpha, tiles_per_seq):
    tm = x1_ref.shape[0]
    tile = pl.program_id(0) % tiles_per_seq
    keep = tile > 0
    uext_ref[0:HALO_ROWS, :] = jnp.where(keep, uh_ref[...], 0.0)
    uext_ref[HALO_ROWS:, :] = u_ref[...]
    cext_ref[0:HALO_ROWS, :] = jnp.where(keep, ch_ref[...], 0.0)
    cext_ref[HALO_ROWS:, :] = c_ref[...]

    def shifted(ext_ref, copies_ref):
        for b in range(8):
            copies_ref[b] = ext_ref[8 - b:HALO_ROWS + tm - b, :]
        return copies_ref

    def back(copies_ref, k):
        a, b = divmod(k, 8)
        off = HALO_ROWS - 8 - 8 * a
        return copies_ref[b, off:off + tm, :]

    cur = u_ref[...]
    u_back = shifted(uext_ref, ushift_ref)
    wins = {}
    win = cur
    for k in range(1, max(POOL_WINDOWS)):
        win = win + back(u_back, k)
        if k + 1 in POOL_WINDOWS:
            wins[k + 1] = win
    pos = tile * tm + lax.broadcasted_iota(jnp.int32, cur.shape, 0)
    width = jnp.left_shift(2, lax.broadcasted_iota(jnp.int32, cur.shape, 1) // POOL_GROUP_DIM)
    count = jnp.minimum(pos + 1, width).astype(F32)
    pooled = _pool_select(wins[2], wins[4], wins[8], wins[16], cur, count)

    c_back = shifted(cext_ref, cshift_ref)
    y = None
    for j in range(CONV_SIZE):
        term = back(c_back, CONV_PAD - j) * cw_ref[j:j + 1, :]
        y = term if y is None else y + term
    conv_act = _conv_post(y, cb_ref, cg_ref, cbeta_ref)

    o_ref[...] = _merge_out(x1_ref[...], attn_ref[...], pooled, conv_act, wao_ref, wpool_ref, ps_ref,
                            wco_ref, wg_ref, wo_ref, g_ref, b_ref, alpha)


def _mix_sample_kernel(x1_ref, attn_ref, u_ref, sp_ref, c_ref, sc_ref, cw_ref, cb_ref, cg_ref, cbeta_ref,
                       wao_ref, wpool_ref, ps_ref, wco_ref, wg_ref, wo_ref, g_ref, b_ref,
                       o_ref, *, alpha, past):
    cur = u_ref[...]
    wins = {}
    win = cur
    for k in range(1, max(POOL_WINDOWS)):
        win = win + sp_ref[POOL_PAD - k]
        if k + 1 in POOL_WINDOWS:
            wins[k + 1] = win
    width = jnp.left_shift(2, lax.broadcasted_iota(jnp.int32, cur.shape, 1) // POOL_GROUP_DIM)
    count = jnp.minimum(past + 1, width).astype(F32)
    pooled = _pool_select(wins[2], wins[4], wins[8], wins[16], cur, count)

    y = c_ref[...] * cw_ref[CONV_PAD:CONV_SIZE, :]
    for j in range(CONV_PAD):
        y = y + sc_ref[j] * cw_ref[j:j + 1, :]
    conv_act = _conv_post(y, cb_ref, cg_ref, cbeta_ref)

    o_ref[...] = _merge_out(x1_ref[...], attn_ref[...], pooled, conv_act, wao_ref, wpool_ref, ps_ref,
                            wco_ref, wg_ref, wo_ref, g_ref, b_ref, alpha)


_MIX_WEIGHTS = ('conv_w', 'conv_b', 'conv_ln_g', 'conv_ln_b', 'w_attn_out', 'w_pool', 'pool_scale',
                'w_conv_out', 'w_gates', 'w_o', 'ln2_g', 'ln2_b')


def _mix_prompt_call(x1, attn, u, glu, w, layer, alpha, seq):
    n, d = x1.shape
    tm = _row_tile(seq, 256)
    assert tm % HALO_ROWS == 0 and HALO_ROWS >= CONV_PAD
    row = lambda width: pl.BlockSpec((tm, width), lambda i: (i, 0))
    halo = lambda width: pl.BlockSpec(
        (HALO_ROWS, width), lambda i: (jnp.maximum(i * (tm // HALO_ROWS) - 1, 0), 0))
    weights = [w[name] for name in _MIX_WEIGHTS]
    return pl.pallas_call(
        functools.partial(_mix_prompt_kernel, alpha=alpha, tiles_per_seq=seq // tm),
        grid=(n // tm,),
        in_specs=[row(d), row(ATTN_WIDTH), row(POOL_WIDTH), halo(POOL_WIDTH), row(CONV_WIDTH), halo(CONV_WIDTH)]
        + [_resident(x.shape, layer) for x in weights],
        out_specs=row(d),
        out_shape=jax.ShapeDtypeStruct((n, d), F32),
        scratch_shapes=[pltpu.VMEM((HALO_ROWS + tm, POOL_WIDTH), F32),
                        pltpu.VMEM((HALO_ROWS + tm, CONV_WIDTH), F32),
                        pltpu.VMEM((8, HALO_ROWS + tm - 8, POOL_WIDTH), F32),
                        pltpu.VMEM((8, HALO_ROWS + tm - 8, CONV_WIDTH), F32)],
        compiler_params=pltpu.CompilerParams(dimension_semantics=("parallel",), vmem_limit_bytes=VMEM_LIMIT),
        name='mix_prompt')(x1, attn, u, u, glu, glu, *weights)


def _mix_sample_call(x1, attn, u, state_pool_t, glu, state_conv_t, w, layer, alpha, past):
    n, d = x1.shape
    full = lambda width: pl.BlockSpec((n, width), lambda i: (0, 0))
    state = lambda x: pl.BlockSpec((None,) + x.shape[1:], lambda i: (layer, 0, 0, 0))
    weights = [w[name] for name in _MIX_WEIGHTS]
    return pl.pallas_call(
        functools.partial(_mix_sample_kernel, alpha=alpha, past=past),
        grid=(1,),
        in_specs=[full(d), full(ATTN_WIDTH), full(POOL_WIDTH), state(state_pool_t),
                  full(CONV_WIDTH), state(state_conv_t)]
        + [_resident(x.shape, layer) for x in weights],
        out_specs=full(d),
        out_shape=jax.ShapeDtypeStruct((n, d), F32),
        compiler_params=pltpu.CompilerParams(dimension_semantics=("arbitrary",), vmem_limit_bytes=VMEM_LIMIT),
        name='mix_sample')(x1, attn, u, state_pool_t, glu, state_conv_t, *weights)


def kernel(x_prompt, x_sample, cache_k, cache_v, state_pool, state_conv, page_table, ln1_g, ln1_b, ffn1_gu, ffn1_down, w_in, sb_bias, w_attn_out, w_pool, pool_scale, conv_w, conv_b, conv_ln_g, conv_ln_b, w_conv_out, w_o, ln2_g, ln2_b, ffn2_gu, ffn2_down, ln3_g, ln3_b):
    depth, d_model = ln1_g.shape
    batch, seq, _ = x_prompt.shape
    n_seq, dec_seq, _ = x_sample.shape
    assert dec_seq == 1 and n_seq == cache_k.shape[2] == PAGE
    n_pages = page_table.shape[1]
    n_phys, page = cache_k.shape[1], cache_k.shape[2]
    past = n_pages * page
    alpha = (2 * depth) ** 0.25
    vec = lambda a: a.reshape(depth, 1, -1)
    w = dict(
        ffn1_gu=ffn1_gu.astype(BF16), ffn1_down=ffn1_down.astype(BF16),
        ffn2_gu=ffn2_gu.astype(BF16), ffn2_down=ffn2_down.astype(BF16),
        w_proj=w_in[:, :, :C_END].astype(BF16), w_gates=w_in[:, :, C_END:].astype(BF16),
        w_attn_out=w_attn_out.astype(BF16), w_conv_out=w_conv_out.astype(BF16), w_o=w_o.astype(BF16),
        w_pool=w_pool.reshape(depth, POOL_WIDTH, -1).astype(BF16),
        conv_w=conv_w.reshape(depth, CONV_SIZE, CONV_WIDTH),
        ln1_g=vec(ln1_g), ln1_b=vec(ln1_b), ln2_g=vec(ln2_g), ln2_b=vec(ln2_b),
        ln3_g=vec(ln3_g), ln3_b=vec(ln3_b), pool_scale=vec(pool_scale),
        conv_b=vec(conv_b), conv_ln_g=vec(conv_ln_g), conv_ln_b=vec(conv_ln_b))
    cache_kt = cache_k.transpose(0, 1, 3, 4, 2).reshape(depth, n_phys, ATTN_WIDTH, page)
    cache_vt = cache_v.transpose(0, 1, 3, 4, 2).reshape(depth, n_phys, ATTN_WIDTH, page)
    state_pool_t = state_pool.transpose(0, 2, 1, 3)
    state_conv_t = state_conv.transpose(0, 2, 1, 3)

    def pages_out(x_t, lead):
        x = x_t.reshape(*lead, N_HEADS, HEAD_DIM, page)
        return jnp.moveaxis(x, -1, -3)

    yp = x_prompt.reshape(batch * seq, d_model)
    ys = x_sample.reshape(n_seq, d_model)
    outs = [[] for _ in range(8)]
    kt_stack = jnp.zeros((depth, batch * seq // page, ATTN_WIDTH, page), F32)
    vt_stack = jnp.zeros_like(kt_stack)
    for l in range(depth):
        x1, _, qn, kb, vb, kt_stack, vt_stack, u, glu = _ffn_call(
            yp, w, l, 'ffn1', alpha, True, (kt_stack, vt_stack))
        attn = _attn_prompt_call(qn, kb, vb, sb_bias[l], batch, seq)
        x2 = _mix_prompt_call(x1, attn, u, glu, w, l, alpha, seq)
        yp = _ffn_call(x2, w, l, 'ffn2', alpha, False)
        outs[2].append(u.reshape(batch, seq, POOL_WIDTH)[:, seq - POOL_PAD:])
        outs[3].append(glu.reshape(batch, seq, CONV_WIDTH)[:, seq - CONV_PAD:])

        x1, qs, _, _, _, kt, vt, u, glu = _ffn_call(ys, w, l, 'ffn1', alpha, True)
        attn = _attn_sample_call(qs, cache_kt, cache_vt, page_table, sb_bias[l], l)
        x2 = _mix_sample_call(x1, attn, u, state_pool_t, glu, state_conv_t, w, l, alpha, past)
        ys = _ffn_call(x2, w, l, 'ffn2', alpha, False)
        outs[4].append(pages_out(kt, (1,)).reshape(n_seq, 1, N_HEADS, HEAD_DIM))
        outs[5].append(pages_out(vt, (1,)).reshape(n_seq, 1, N_HEADS, HEAD_DIM))
        outs[6].append(jnp.concatenate([state_pool[l][:, 1:], u[:, None]], axis=1))
        outs[7].append(jnp.concatenate([state_conv[l][:, 1:], glu[:, None]], axis=1))
    stacked = [pages_out(kt_stack, (depth, batch, seq // page)), pages_out(vt_stack, (depth, batch, seq // page))]
    stacked += [jnp.stack(o) for o in outs[2:]]
    return (yp.reshape(batch, seq, d_model), ys.reshape(n_seq, 1, d_model), *stacked)
```
